```python
import jax, jax.numpy as jnp
from jax import lax
import numpy as np


D_MODEL = 1024
BATCH = 8
SEQ = 2048
DEPTH = 2
DEC_BATCH = 128
DEC_SEQ = 8
PAST_LEN = 16384
PAGE_SIZE = 128

MIX_WIDTH = D_MODEL
SGU_HEADS = 4
SGU_WIDTH = MIX_WIDTH // 2
SGU_HEAD_DIM = SGU_WIDTH // SGU_HEADS
CHUNK = 128
POOL_WINDOWS = (2, 4, 8, 16)
POOL_GROUPS = len(POOL_WINDOWS)
POOL_WIDTH = MIX_WIDTH - SGU_WIDTH
POOL_GROUP_DIM = POOL_WIDTH // POOL_GROUPS
POOL_BUF = max(POOL_WINDOWS) - 1
IN_WIDTH = 2 * SGU_WIDTH + POOL_WIDTH
D_FF = 7 * D_MODEL // 2
N_EXPERTS = 8
TOP_K = 2
N_DENSE = (DEPTH + 1) // 2
N_MOE = DEPTH // 2
EPS = 1e-6

kernel_name = 'hybrid_sgu_pool_adaln_decoder_step'


def rmsnorm(x, g):
    xf = x.astype(jnp.float32)
    xf = xf * lax.rsqrt(jnp.mean(xf * xf, axis=-1, keepdims=True) + EPS)
    return xf.astype(x.dtype) * g


def head_layernorm(v, g, b):
    B, L, _ = v.shape
    vf = v.astype(jnp.float32).reshape(B, L, SGU_HEADS, SGU_HEAD_DIM)
    mu = jnp.mean(vf, axis=-1, keepdims=True)
    var = jnp.mean(jnp.square(vf - mu), axis=-1, keepdims=True)
    vn = ((vf - mu) * lax.rsqrt(var + EPS)).reshape(B, L, SGU_WIDTH)
    return vn.astype(v.dtype) * g + b


def adaln_params(c, w, b):
    mod = jax.nn.silu(c) @ w + b
    return jnp.split(mod[:, None, :], 6, axis=-1)


def modulate(x, g, shift, scale):
    return rmsnorm(x, g) * (1 + scale) + shift


def chunk_sgu(u, v_raw, g, b, sgu_w, sgu_b, chunk):
    B, L, _ = v_raw.shape
    v = head_layernorm(v_raw, g, b)
    vc = v.reshape(B, L // chunk, chunk, SGU_HEADS, SGU_HEAD_DIM)
    mask = jnp.tril(jnp.ones((chunk, chunk), dtype=bool))
    w = jnp.where(mask[None], sgu_w[:, :chunk, :chunk], 0).astype(v.dtype)
    bias = sgu_b[:, :chunk].T[None, None, :, :, None]
    mixed = jnp.einsum('hts,bnshd->bnthd', w, vc) + bias
    y = u * mixed.reshape(B, L, SGU_WIDTH)
    return y, v


def pool_mix(buf, p, pos, pool_w, pool_scale):
    B, L, _ = p.shape
    xpad = jnp.concatenate([buf.astype(p.dtype), p], axis=1)
    cs = jnp.cumsum(xpad.astype(jnp.float32), axis=1)
    cs = jnp.concatenate([jnp.zeros((B, 1, POOL_WIDTH), jnp.float32), cs], axis=1)
    pf = p.astype(jnp.float32)
    outs = []
    for gi, win in enumerate(POOL_WINDOWS):
        lo, hi = gi * POOL_GROUP_DIM, (gi + 1) * POOL_GROUP_DIM
        wsum = cs[:, POOL_BUF + 1:, lo:hi] - cs[:, POOL_BUF + 1 - win:POOL_BUF + 1 - win + L, lo:hi]
        cnt = jnp.minimum(win, pos + 1).astype(jnp.float32)[None, :, None]
        outs.append(wsum / cnt - pf[:, :, lo:hi])
    d = jnp.concatenate(outs, axis=-1).astype(p.dtype).reshape(B, L, POOL_GROUPS, POOL_GROUP_DIM)
    y = jnp.einsum('blgc,gcd->blgd', d, pool_w).reshape(B, L, POOL_WIDTH) * pool_scale
    return y, xpad[:, -POOL_BUF:]


def swiglu(h, w1, w3, w2):
    return (jax.nn.silu(h @ w1) * (h @ w3)) @ w2


def moe_swiglu(h, router_w, w1, w3, w2):
    B, L, D = h.shape
    t = h.reshape(B * L, D)
    logits = (t @ router_w).astype(jnp.float32)
    top_v, top_i = lax.top_k(logits, TOP_K)
    gates = jax.nn.softmax(top_v, axis=-1)
    out = jnp.zeros((B * L, D), jnp.float32)
    for e in range(N_EXPERTS):
        w_e = jnp.sum(jnp.where(top_i == e, gates, 0.0), axis=-1)
        out = out + w_e[:, None] * swiglu(t, w1[e], w3[e], w2[e]).astype(jnp.float32)
    return out.astype(h.dtype).reshape(B, L, D)


def setup_inputs(seed: int = 0) -> dict:
    key = jax.random.key(seed)
    ks = jax.random.split(key, 32)
    f32 = jnp.float32
    nrm = lambda k, shape, s: jax.random.normal(k, shape, f32) * s
    gain = lambda k, shape: 1.0 + nrm(k, shape, 0.02)
    return {
        'x_prompt': nrm(ks[0], (BATCH, SEQ, D_MODEL), 1.0),
        'x_sample': nrm(ks[1], (DEC_BATCH, DEC_SEQ, D_MODEL), 1.0),
        'state_pool': nrm(ks[2], (DEPTH, DEC_BATCH, POOL_BUF, POOL_WIDTH), 1.0),
        'c_prompt': nrm(ks[3], (BATCH, D_MODEL), 1.0),
        'c_sample': nrm(ks[4], (DEC_BATCH, D_MODEL), 1.0),
        'norm1_g': gain(ks[5], (DEPTH, D_MODEL)),
        'norm2_g': gain(ks[6], (DEPTH, D_MODEL)),
        'ada_w': nrm(ks[7], (DEPTH, D_MODEL, 6 * D_MODEL), 0.5 * D_MODEL ** -0.5),
        'ada_b': nrm(ks[8], (DEPTH, 6 * D_MODEL), 0.02),
        'w_in': nrm(ks[9], (DEPTH, D_MODEL, IN_WIDTH), D_MODEL ** -0.5),
        'v_norm_g': gain(ks[10], (DEPTH, SGU_WIDTH)),
        'v_norm_b': nrm(ks[11], (DEPTH, SGU_WIDTH), 0.02),
        'sgu_w': nrm(ks[12], (DEPTH, SGU_HEADS, CHUNK, CHUNK), CHUNK ** -0.5),
        'sgu_b': 1.0 + nrm(ks[13], (DEPTH, SGU_HEADS, CHUNK), 0.1),
        'pool_w': nrm(ks[14], (DEPTH, POOL_GROUPS, POOL_GROUP_DIM, POOL_GROUP_DIM), POOL_GROUP_DIM ** -0.5),
        'pool_scale': 1.0 + nrm(ks[15], (DEPTH, POOL_WIDTH), 0.1),
        'branch_a_g': gain(ks[16], (DEPTH, SGU_WIDTH)),
        'branch_b_g': gain(ks[17], (DEPTH, POOL_WIDTH)),
        'w_out': nrm(ks[18], (DEPTH, MIX_WIDTH, D_MODEL), MIX_WIDTH ** -0.5),
        'ffn_w1': nrm(ks[19], (N_DENSE, D_MODEL, D_FF), D_MODEL ** -0.5),
        'ffn_w3': nrm(ks[20], (N_DENSE, D_MODEL, D_FF), D_MODEL ** -0.5),
        'ffn_w2': nrm(ks[21], (N_DENSE, D_FF, D_MODEL), D_FF ** -0.5),
        'router_w': nrm(ks[22], (N_MOE, D_MODEL, N_EXPERTS), D_MODEL ** -0.5),
        'moe_w1': nrm(ks[23], (N_MOE, N_EXPERTS, D_MODEL, D_FF), D_MODEL ** -0.5),
        'moe_w3': nrm(ks[24], (N_MOE, N_EXPERTS, D_MODEL, D_FF), D_MODEL ** -0.5),
        'moe_w2': nrm(ks[25], (N_MOE, N_EXPERTS, D_FF, D_MODEL), D_FF ** -0.5),
        'final_g': gain(ks[26], (D_MODEL,)),
    }


def reference(x_prompt, x_sample, state_pool, c_prompt, c_sample, norm1_g, norm2_g, ada_w, ada_b,
              w_in, v_norm_g, v_norm_b, sgu_w, sgu_b, pool_w, pool_scale, branch_a_g, branch_b_g,
              w_out, ffn_w1, ffn_w3, ffn_w2, router_w, moe_w1, moe_w3, moe_w2, final_g):
    pos_p = jnp.arange(x_prompt.shape[1], dtype=jnp.int32)
    pos_s = PAST_LEN + jnp.arange(x_sample.shape[1], dtype=jnp.int32)
    buf_p0 = jnp.zeros((x_prompt.shape[0], POOL_BUF, POOL_WIDTH), x_prompt.dtype)

    def run_layer(l, x, c, buf, pos, chunk):
        sh1, sc1, gt1, sh2, sc2, gt2 = adaln_params(c, ada_w[l], ada_b[l])
        h = modulate(x, norm1_g[l], sh1, sc1)
        z = h @ w_in[l]
        u = jax.nn.gelu(z[..., :SGU_WIDTH])
        v_raw = jax.nn.gelu(z[..., SGU_WIDTH:2 * SGU_WIDTH])
        p = z[..., 2 * SGU_WIDTH:]
        ya, v = chunk_sgu(u, v_raw, v_norm_g[l], v_norm_b[l], sgu_w[l], sgu_b[l], chunk)
        yb, new_buf = pool_mix(buf, p, pos, pool_w[l], pool_scale[l])
        mix = jnp.concatenate([rmsnorm(ya, branch_a_g[l]), rmsnorm(yb, branch_b_g[l])], axis=-1) @ w_out[l]
        x = x + gt1 * mix
        h = modulate(x, norm2_g[l], sh2, sc2)
        if l % 2 == 0:
            f = swiglu(h, ffn_w1[l // 2], ffn_w3[l // 2], ffn_w2[l // 2])
        else:
            f = moe_swiglu(h, router_w[l // 2], moe_w1[l // 2], moe_w3[l // 2], moe_w2[l // 2])
        x = x + gt2 * f
        return x, new_buf, v

    xp, xs = x_prompt, x_sample
    bufs_p, bufs_s, vs_p, vs_s = [], [], [], []
    for l in range(DEPTH):
        xp, bp, vp = run_layer(l, xp, c_prompt, buf_p0, pos_p, CHUNK)
        xs, bs, vsm = run_layer(l, xs, c_sample, state_pool[l], pos_s, x_sample.shape[1])
        bufs_p.append(bp)
        bufs_s.append(bs)
        vs_p.append(vp[:, -CHUNK:])
        vs_s.append(vsm)
    y_prompt = rmsnorm(xp, final_g)
    y_sample = rmsnorm(xs, final_g)
    pool_state_prompt = jnp.stack(bufs_p, axis=0)
    pool_state_sample = jnp.stack(bufs_s, axis=0)
    chunk_v_prompt = jnp.stack(vs_p, axis=0)
    chunk_v_sample = jnp.stack(vs_s, axis=0)
    return (y_prompt, y_sample, pool_state_prompt, pool_state_sample, chunk_v_prompt, chunk_v_sample)
```

```python
import functools

import jax
import jax.numpy as jnp
from jax import lax
from jax.experimental import pallas as pl
from jax.experimental.pallas import tpu as pltpu

F32 = jnp.float32
BF16 = jnp.bfloat16
I32 = jnp.int32

D = 1024
SGU_W = 512
POOL_W = 512
HEADS = 4
HEAD_D = 128
CHUNK = 128
WINDOWS = (2, 4, 8, 16)
POOL_BUF = 15
HALO = 16
IN_W = 2 * SGU_W + POOL_W
D_FF = 3584
N_EXP = 8
EPS = 1e-6
PAST_LEN = 16384

LANE = 128
TF = 512
NF = D_FF // TF
SUBM = 256
SPT = 8
VMEM_LIMIT = 56 * 1024 * 1024


def _rmsn(x):
    return x * lax.rsqrt(jnp.mean(x * x, axis=-1, keepdims=True) + EPS)


def _expand(m, reps):
    return m if reps == 1 else jnp.concatenate([m] * reps, axis=0)


def _mod_chunk(mod_ref, i, reps):
    return _expand(mod_ref[:, i * D:(i + 1) * D], reps)


def _dot(a, b):
    return jnp.dot(a, b, preferred_element_type=F32)


def _adaln_body(c_ref, w_ref, b_ref, o_ref):
    s = jax.nn.silu(c_ref[...]).astype(BF16)
    o_ref[...] = _dot(s, w_ref[...].astype(BF16)) + b_ref[...]


def _adaln(c_all, ada_w, ada_b):
    depth, _, n_out = ada_w.shape
    rows = c_all.shape[0]
    tn = 1024
    return pl.pallas_call(
        _adaln_body,
        out_shape=jax.ShapeDtypeStruct((depth, rows, n_out), F32),
        grid=(depth, n_out // tn),
        in_specs=[
            pl.BlockSpec((rows, D), lambda l, n: (0, 0)),
            pl.BlockSpec((None, D, tn), lambda l, n: (l, 0, n)),
            pl.BlockSpec((None, 1, tn), lambda l, n: (l, 0, n)),
        ],
        out_specs=pl.BlockSpec((None, rows, tn), lambda l, n: (l, 0, n)),
        compiler_params=pltpu.CompilerParams(
            dimension_semantics=("arbitrary", "arbitrary"), vmem_limit_bytes=VMEM_LIMIT),
        name="adaln",
    )(c_all, ada_w, ada_b.reshape(depth, 1, n_out))


def _mixer_front(x, mod_ref, g1_ref, win_ref, vg_ref, vb_ref, reps):
    sh1 = _mod_chunk(mod_ref, 0, reps)
    sc1 = _mod_chunk(mod_ref, 1, reps)
    h = _rmsn(x) * g1_ref[...] * (1.0 + sc1) + sh1
    z = _dot(h.astype(BF16), win_ref[...])
    u = jax.nn.gelu(z[:, :SGU_W])
    vr = jax.nn.gelu(z[:, SGU_W:2 * SGU_W])
    p = z[:, 2 * SGU_W:]
    vs = []
    for hh in range(HEADS):
        vh = vr[:, hh * HEAD_D:(hh + 1) * HEAD_D]
        dlt = vh - jnp.mean(vh, axis=-1, keepdims=True)
        var = jnp.mean(dlt * dlt, axis=-1, keepdims=True)
        vs.append(dlt * lax.rsqrt(var + EPS))
    v = jnp.concatenate(vs, axis=-1) * vg_ref[...] + vb_ref[...]
    return u, v, p


def _mixer_back(x, ya, d, mod_ref, g2_ref, pw_ref, ps_ref, ga_ref, gb_ref, wout_ref, reps):
    db = d.astype(BF16)
    yb = jnp.concatenate(
        [_dot(db[:, :2 * LANE], pw_ref[0]), _dot(db[:, 2 * LANE:], pw_ref[1])], axis=-1) * ps_ref[...]
    mixin = jnp.concatenate([_rmsn(ya) * ga_ref[...], _rmsn(yb) * gb_ref[...]], axis=-1)
    mix = _dot(mixin.astype(BF16), wout_ref[...])
    x1 = x + _mod_chunk(mod_ref, 2, reps) * mix
    sh2 = _mod_chunk(mod_ref, 3, reps)
    sc2 = _mod_chunk(mod_ref, 4, reps)
    h2 = _rmsn(x1) * g2_ref[...] * (1.0 + sc2) + sh2
    return x1, h2


def _route(h2, rw_ref, u_ref, cnt_scr, ri_ref, rg_ref):
    t = h2.shape[0]
    logits = lax.dot_general(rw_ref[...], h2.astype(BF16), (((1,), (1,)), ((), ())),
                             preferred_element_type=F32)[:N_EXP]
    sub = lax.broadcasted_iota(I32, (N_EXP, t), 0).astype(F32)
    m1 = jnp.max(logits, axis=0, keepdims=True)
    i1 = jnp.min(jnp.where(logits == m1, sub, float(N_EXP)), axis=0, keepdims=True)
    rest = jnp.where(sub == i1, -jnp.inf, logits)
    m2 = jnp.max(rest, axis=0, keepdims=True)
    i2 = jnp.min(jnp.where(rest == m2, sub, float(N_EXP)), axis=0, keepdims=True)
    e2 = jnp.exp(m2 - m1)
    den = 1.0 + e2
    g1 = 1.0 / den
    g2 = e2 / den
    sub16 = lax.broadcasted_iota(I32, (2 * N_EXP, t), 0).astype(F32)
    mask16 = jnp.where((sub16 == i1) | (sub16 == i2), 1.0, 0.0)
    rank = _dot(mask16.astype(BF16), u_ref[...])[:N_EXP] + cnt_scr[:, 0:1]
    cnt_scr[...] = cnt_scr[...] + jnp.sum(mask16[:N_EXP], axis=1, keepdims=True)
    r1 = jnp.sum(jnp.where(sub == i1, rank, 0.0), axis=0, keepdims=True)
    r2 = jnp.sum(jnp.where(sub == i2, rank, 0.0), axis=0, keepdims=True)
    zf = jnp.zeros((N_EXP, t), F32)
    ri = jnp.where(sub == 0.0, i1, jnp.where(sub == 1.0, i2, jnp.where(
        sub == 2.0, r1, jnp.where(sub == 3.0, r2, zf))))
    ri_ref[...] = ri.astype(I32)
    rg_ref[...] = jnp.where(sub == 0.0, g1, jnp.where(sub == 1.0, g2, zf))


def _mixer_prompt_body(*refs, T, n_tiles, emit_route):
    (x_ref, mod_ref, g1_ref, g2_ref, win_ref, vg_ref, vb_ref, sw_ref, sb_ref,
     pw_ref, ps_ref, ga_ref, gb_ref, wout_ref) = refs[:14]
    k = 14
    if emit_route:
        rw_ref, cin_ref, u_ref = refs[k:k + 3]
        k += 3
    x1_ref = refs[k]
    k += 1
    if emit_route:
        h2_ref, ri_ref, rg_ref, cout_ref = refs[k:k + 4]
        k += 4
    pstate_ref, cv_ref, pbuf = refs[k:k + 3]
    k += 3
    cnt_scr = refs[k] if emit_route else None

    b = pl.program_id(0)
    j = pl.program_id(1)
    x = x_ref[...]
    u, v, p = _mixer_front(x, mod_ref, g1_ref, win_ref, vg_ref, vb_ref, 1)

    nc = T // CHUNK
    row = lax.broadcasted_iota(I32, (CHUNK, CHUNK), 0)
    col = lax.broadcasted_iota(I32, (CHUNK, CHUNK), 1)
    vb16 = v.astype(BF16)
    heads = []
    for hh in range(HEADS):
        w = jnp.where(row >= col, sw_ref[hh], 0.0).astype(BF16)
        vcat = jnp.concatenate(
            [vb16[c * CHUNK:(c + 1) * CHUNK, hh * HEAD_D:(hh + 1) * HEAD_D] for c in range(nc)], axis=1)
        heads.append(_dot(w, vcat))
    mixed = jnp.concatenate(
        [jnp.concatenate([heads[hh][:, c * HEAD_D:(c + 1) * HEAD_D] for hh in range(HEADS)], axis=1)
         for c in range(nc)], axis=0)
    ya = u * (mixed + _expand(sb_ref[...], nc))

    @pl.when(j == 0)
    def _():
        pbuf[0:HALO, :] = jnp.zeros((HALO, POOL_W), F32)

    pbuf[HALO:HALO + T, :] = p
    pos = j * T + lax.broadcasted_iota(I32, (T, 1), 0)
    ds = []
    for g, win in enumerate(WINDOWS):
        lo = g * LANE
        ws = p[:, lo:lo + LANE]
        for kk in range(1, win):
            ws = ws + pbuf[HALO - kk:HALO - kk + T, lo:lo + LANE]
        cnt = jnp.minimum(win, pos + 1).astype(F32)
        ds.append(ws / cnt - p[:, lo:lo + LANE])
    d = jnp.concatenate(ds, axis=-1)
    pbuf[0:HALO, :] = p[T - HALO:, :]

    x1, h2 = _mixer_back(x, ya, d, mod_ref, g2_ref, pw_ref, ps_ref, ga_ref, gb_ref, wout_ref, 1)
    x1_ref[...] = x1

    @pl.when(j == n_tiles - 1)
    def _():
        pstate_ref[...] = p[T - HALO:, :]
        cv_ref[...] = v[T - CHUNK:, :]

    if emit_route:
        @pl.when((b == 0) & (j == 0))
        def _():
            cnt_scr[...] = cin_ref[...]

        h2_ref[...] = h2
        _route(h2, rw_ref, u_ref, cnt_scr, ri_ref, rg_ref)
        cout_ref[...] = cnt_scr[...]


def _mixer_sample_body(*refs, n_seq, n_pos, emit_route):
    (x_ref, mod_ref, buf_ref, g1_ref, g2_ref, win_ref, vg_ref, vb_ref, sw_ref, sb_ref,
     pw_ref, ps_ref, ga_ref, gb_ref, wout_ref) = refs[:15]
    k = 15
    if emit_route:
        rw_ref, cin_ref, u_ref = refs[k:k + 3]
        k += 3
    x1_ref = refs[k]
    k += 1
    if emit_route:
        h2_ref, ri_ref, rg_ref, cout_ref = refs[k:k + 4]
        k += 4
    pstate_ref, cv_ref = refs[k:k + 2]
    k += 2
    cnt_scr = refs[k] if emit_route else None

    reps = n_pos
    x = x_ref[...]
    u, v, p = _mixer_front(x, mod_ref, g1_ref, win_ref, vg_ref, vb_ref, reps)

    def slab(a, t):
        return a[t * n_seq:(t + 1) * n_seq, :]

    mixed = []
    for t in range(n_pos):
        acc = sb_ref[t:t + 1, :] + sw_ref[t * n_pos:t * n_pos + 1, :] * slab(v, 0)
        for s in range(1, t + 1):
            acc = acc + sw_ref[t * n_pos + s:t * n_pos + s + 1, :] * slab(v, s)
        mixed.append(acc)
    ya = u * jnp.concatenate(mixed, axis=0)

    hist = [buf_ref[jj] for jj in range(POOL_BUF)] + [slab(p, t) for t in range(n_pos)]
    drows = []
    for t in range(n_pos):
        parts = []
        for g, win in enumerate(WINDOWS):
            lo = g * LANE
            ws = hist[POOL_BUF + t][:, lo:lo + LANE]
            for kk in range(1, win):
                ws = ws + hist[POOL_BUF + t - kk][:, lo:lo + LANE]
            cnt = float(min(win, PAST_LEN + t + 1))
            parts.append(ws / cnt - hist[POOL_BUF + t][:, lo:lo + LANE])
        drows.append(jnp.concatenate(parts, axis=-1))
    d = jnp.concatenate(drows, axis=0)
    for jj in range(POOL_BUF):
        pstate_ref[jj] = hist[n_pos + jj]
    for t in range(n_pos):
        cv_ref[t] = slab(v, t)

    x1, h2 = _mixer_back(x, ya, d, mod_ref, g2_ref, pw_ref, ps_ref, ga_ref, gb_ref, wout_ref, reps)
    x1_ref[...] = x1

    if emit_route:
        cnt_scr[...] = cin_ref[...]
        h2_ref[...] = h2
        _route(h2, rw_ref, u_ref, cnt_scr, ri_ref, rg_ref)
        cout_ref[...] = cnt_scr[...]


def _const_spec(shape):
    nd = len(shape)
    return pl.BlockSpec(shape, lambda *_: (0,) * nd)


def _mixer_prompt(x, mod, lw, route_w=None, cnt_in=None, *, T=512):
    n_b = mod.shape[0]
    n_tok = x.shape[0]
    seq = n_tok // n_b
    n_tiles = seq // T
    emit_route = route_w is not None
    tok_map = lambda b, j: (b * n_tiles + j, 0)
    in_specs = [
        pl.BlockSpec((T, D), tok_map),
        pl.BlockSpec((None, 1, 6 * D), lambda b, j: (b, 0, 0)),
    ] + [_const_spec(w.shape) for w in lw]
    args = [x, mod.reshape(n_b, 1, 6 * D)] + list(lw)
    out_shape = [jax.ShapeDtypeStruct((n_tok, D), F32)]
    out_specs = [pl.BlockSpec((T, D), tok_map)]
    scratch = [pltpu.VMEM((HALO + T, POOL_W), F32)]
    if emit_route:
        umat = jnp.triu(jnp.ones((T, T), BF16), 1)
        in_specs += [_const_spec(route_w.shape), _const_spec(cnt_in.shape), _const_spec(umat.shape)]
        args += [route_w, cnt_in, umat]
        out_shape += [jax.ShapeDtypeStruct((n_tok, D), F32),
                      jax.ShapeDtypeStruct((N_EXP, n_tok), I32),
                      jax.ShapeDtypeStruct((N_EXP, n_tok), F32),
                      jax.ShapeDtypeStruct((N_EXP, LANE), F32)]
        out_specs += [pl.BlockSpec((T, D), tok_map),
                      pl.BlockSpec((N_EXP, T), lambda b, j: (0, b * n_tiles + j)),
                      pl.BlockSpec((N_EXP, T), lambda b, j: (0, b * n_tiles + j)),
                      _const_spec((N_EXP, LANE))]
        scratch += [pltpu.VMEM((N_EXP, LANE), F32)]
    out_shape += [jax.ShapeDtypeStruct((n_b, HALO, POOL_W), F32),
                  jax.ShapeDtypeStruct((n_b, CHUNK, SGU_W), F32)]
    out_specs += [pl.BlockSpec((None, HALO, POOL_W), lambda b, j: (b, 0, 0)),
                  pl.BlockSpec((None, CHUNK, SGU_W), lambda b, j: (b, 0, 0))]
    return pl.pallas_call(
        functools.partial(_mixer_prompt_body, T=T, n_tiles=n_tiles, emit_route=emit_route),
        out_shape=out_shape,
        grid=(n_b, n_tiles),
        in_specs=in_specs,
        out_specs=out_specs,
        scratch_shapes=scratch,
        compiler_params=pltpu.CompilerParams(
            dimension_semantics=("arbitrary", "arbitrary"), vmem_limit_bytes=VMEM_LIMIT),
        name="mixer_prompt_route" if emit_route else "mixer_prompt",
    )(*args)


def _mixer_sample(x, mod, buf, lw, route_w=None, cnt_in=None):
    n_seq = mod.shape[0]
    n_tok = x.shape[0]
    n_pos = n_tok // n_seq
    emit_route = route_w is not None
    in_specs = [_const_spec(x.shape), _const_spec(mod.shape), _const_spec(buf.shape)]
    in_specs += [_const_spec(w.shape) for w in lw]
    args = [x, mod, buf] + list(lw)
    out_shape = [jax.ShapeDtypeStruct((n_tok, D), F32)]
    out_specs = [_const_spec((n_tok, D))]
    scratch = []
    if emit_route:
        umat = jnp.triu(jnp.ones((n_tok, n_tok), BF16), 1)
        in_specs += [_const_spec(route_w.shape), _const_spec(cnt_in.shape), _const_spec(umat.shape)]
        args += [route_w, cnt_in, umat]
        out_shape += [jax.ShapeDtypeStruct((n_tok, D), F32),
                      jax.ShapeDtypeStruct((N_EXP, n_tok), I32),
                      jax.ShapeDtypeStruct((N_EXP, n_tok), F32),
                      jax.ShapeDtypeStruct((N_EXP, LANE), F32)]
        out_specs += [_const_spec((n_tok, D)), _const_spec((N_EXP, n_tok)),
                      _const_spec((N_EXP, n_tok)), _const_spec((N_EXP, LANE))]
        scratch += [pltpu.VMEM((N_EXP, LANE), F32)]
    out_shape += [jax.ShapeDtypeStruct((POOL_BUF, n_seq, POOL_W), F32),
                  jax.ShapeDtypeStruct((n_pos, n_seq, SGU_W), F32)]
    out_specs += [_const_spec((POOL_BUF, n_seq, POOL_W)), _const_spec((n_pos, n_seq, SGU_W))]
    return pl.pallas_call(
        functools.partial(_mixer_sample_body, n_seq=n_seq, n_pos=n_pos, emit_route=emit_route),
        out_shape=out_shape,
        grid=(1,),
        in_specs=in_specs,
        out_specs=out_specs,
        scratch_shapes=scratch,
        compiler_params=pltpu.CompilerParams(
            dimension_semantics=("arbitrary",), vmem_limit_bytes=VMEM_LIMIT),
        name="mixer_sample_route" if emit_route else "mixer_sample",
    )(*args)


def _ffn_body(x_ref, mod_ref, g2_ref, w1_ref, w3_ref, w2_ref, o_ref, *, reps):
    x1 = x_ref[...]
    sh2 = _mod_chunk(mod_ref, 3, reps)
    sc2 = _mod_chunk(mod_ref, 4, reps)
    h = (_rmsn(x1) * g2_ref[...] * (1.0 + sc2) + sh2).astype(BF16)
    acc = None
    for c in range(NF):
        a = _dot(h, w1_ref[:, c * TF:(c + 1) * TF])
        bb = _dot(h, w3_ref[:, c * TF:(c + 1) * TF])
        part = _dot((jax.nn.silu(a) * bb).astype(BF16), w2_ref[c * TF:(c + 1) * TF, :])
        acc = part if acc is None else acc + part
    o_ref[...] = x1 + _mod_chunk(mod_ref, 5, reps) * acc


def _ffn(x1, mod, g2, w1, w3, w2, *, tm, tok_per_mod):
    n_tok = x1.shape[0]
    if tok_per_mod:
        mod_in = mod.reshape(mod.shape[0], 1, 6 * D)
        mod_spec = pl.BlockSpec((None, 1, 6 * D), lambda i: ((i * tm) // tok_per_mod, 0, 0))
        reps = 1
    else:
        mod_in = mod
        mod_spec = _const_spec(mod.shape)
        reps = tm // mod.shape[0]
    resident = lambda s: pl.BlockSpec(s, lambda i: (0,) * len(s), pipeline_mode=pl.Buffered(1))
    return pl.pallas_call(
        functools.partial(_ffn_body, reps=reps),
        out_shape=jax.ShapeDtypeStruct((n_tok, D), F32),
        grid=(n_tok // tm,),
        in_specs=[pl.BlockSpec((tm, D), lambda i: (i, 0)), mod_spec, _const_spec(g2.shape),
                  resident(w1.shape), resident(w3.shape), resident(w2.shape)],
        out_specs=pl.BlockSpec((tm, D), lambda i: (i, 0)),
        compiler_params=pltpu.CompilerParams(
            dimension_semantics=("arbitrary",), vmem_limit_bytes=VMEM_LIMIT),
        name="ffn_dense",
    )(x1, mod_in, g2, w1, w3, w2)


def _zero_tail(zsrc, dst_ref, tail0, n_rows, sem):
    def blk_copy(i):
        r = pl.multiple_of(tail0 + i * SUBM, SUBM)
        return pltpu.make_async_copy(zsrc, dst_ref.at[pl.ds(r, SUBM)], sem)

    nblk = (n_rows - tail0) // SUBM

    def start(i, c):
        blk_copy(i).start()
        return c
    lax.fori_loop(0, nblk, start, 0)

    def wait(i, c):
        blk_copy(i).wait()
        return c
    lax.fori_loop(0, nblk, wait, 0)


def _scatter_body(off_ref, pad0_ref, padn_ref, tail_ref, rip_ref, ris_ref, hp_ref, hs_ref, xs_ref,
                  zblk, sem, *, ts, n_p_tiles, n_rows):
    step = pl.program_id(0)

    def row_copy(src, dst_row):
        return pltpu.make_async_copy(src, xs_ref.at[pl.ds(dst_row, 1)], sem)

    @pl.when(step == 0)
    def _():
        zblk[...] = jnp.zeros(zblk.shape, F32)
        for e in range(N_EXP):
            def fill(r, c):
                row_copy(zblk.at[pl.ds(0, 1)], pad0_ref[e] + r).start()
                return c
            lax.fori_loop(0, padn_ref[e], fill, 0)

            def drain(r, c):
                row_copy(zblk.at[pl.ds(0, 1)], pad0_ref[e]).wait()
                return c
            lax.fori_loop(0, padn_ref[e], drain, 0)
        _zero_tail(zblk, xs_ref, tail_ref[0], n_rows, sem)

    def scatter_from(ri_ref, h_ref):
        def issue(t, c):
            s1 = off_ref[ri_ref[0, t]] + ri_ref[2, t]
            s2 = off_ref[ri_ref[1, t]] + ri_ref[3, t]
            row_copy(h_ref.at[pl.ds(t, 1)], s1).start()
            row_copy(h_ref.at[pl.ds(t, 1)], s2).start()
            return c
        lax.fori_loop(0, ts, issue, 0)

        def drain_all(t, c):
            row_copy(h_ref.at[pl.ds(0, 1)], 0).wait()
            return c
        lax.fori_loop(0, 2 * ts, drain_all, 0)

    @pl.when(step < n_p_tiles)
    def _():
        scatter_from(rip_ref, hp_ref)

    @pl.when(step >= n_p_tiles)
    def _():
        scatter_from(ris_ref, hs_ref)


def _scatter_rows(h2_p, h2_s, ri_p, ri_s, off, pad0, padn, tail0, n_rows, *, ts=512):
    n_p_tiles = h2_p.shape[0] // ts
    n_s_tiles = h2_s.shape[0] // ts
    p_idx = lambda i: jnp.minimum(i, n_p_tiles - 1)
    s_idx = lambda i: jnp.maximum(i - n_p_tiles, 0)
    grid_spec = pltpu.PrefetchScalarGridSpec(
        num_scalar_prefetch=4,
        grid=(n_p_tiles + n_s_tiles,),
        in_specs=[pl.BlockSpec((N_EXP, ts), lambda i, *_: (0, p_idx(i)), memory_space=pltpu.SMEM),
                  pl.BlockSpec((N_EXP, ts), lambda i, *_: (0, s_idx(i)), memory_space=pltpu.SMEM),
                  pl.BlockSpec((ts, D), lambda i, *_: (p_idx(i), 0)),
                  pl.BlockSpec((ts, D), lambda i, *_: (s_idx(i), 0))],
        out_specs=pl.BlockSpec(memory_space=pl.ANY),
        scratch_shapes=[pltpu.VMEM((SUBM, D), F32), pltpu.SemaphoreType.DMA(())],
    )
    return pl.pallas_call(
        functools.partial(_scatter_body, ts=ts, n_p_tiles=n_p_tiles, n_rows=n_rows),
        out_shape=jax.ShapeDtypeStruct((n_rows, D), F32),
        grid_spec=grid_spec,
        compiler_params=pltpu.CompilerParams(
            dimension_semantics=("arbitrary",), vmem_limit_bytes=VMEM_LIMIT, has_side_effects=True),
        name="moe_scatter",
    )(off, pad0, padn, tail0, ri_p, ri_s, h2_p, h2_s)


def _moe_body(se_ref, row0_ref, nsub_ref, tail_ref, xs_ref, w1_ref, w3_ref, w2_ref, ys_ref,
              xbuf, acc, sem_in, sem_out, *, n_rows):
    g = pl.program_id(0)
    f = pl.program_id(1)
    n = nsub_ref[g]
    row0 = pl.multiple_of(row0_ref[g], SUBM)

    @pl.when((g == 0) & (f == 0))
    def _():
        acc[0:SUBM, :] = jnp.zeros((SUBM, D), F32)
        _zero_tail(acc.at[pl.ds(0, SUBM)], ys_ref, tail_ref[0], n_rows, sem_out)

    @pl.when((f == 0) & (n > 0))
    def _():
        cp = pltpu.make_async_copy(xs_ref.at[pl.ds(row0, SPT * SUBM)], xbuf, sem_in)
        cp.start()
        cp.wait()

    def run(first):
        w1b = w1_ref[...].astype(BF16)
        w3b = w3_ref[...].astype(BF16)
        w2b = w2_ref[...].astype(BF16)

        def sub(i, c):
            r = pl.multiple_of(i * SUBM, SUBM)
            h = xbuf[pl.ds(r, SUBM), :].astype(BF16)
            a = _dot(h, w1b)
            bb = _dot(h, w3b)
            part = _dot((jax.nn.silu(a) * bb).astype(BF16), w2b)
            if first:
                acc[pl.ds(r, SUBM), :] = part
            else:
                acc[pl.ds(r, SUBM), :] = acc[pl.ds(r, SUBM), :] + part
            return c
        lax.fori_loop(0, n, sub, 0)

    @pl.when((f == 0) & (n > 0))
    def _():
        run(True)

    @pl.when((f > 0) & (n > 0))
    def _():
        run(False)

    @pl.when((f == NF - 1) & (n > 0))
    def _():
        def out_copy(i):
            r = pl.multiple_of(i * SUBM, SUBM)
            return pltpu.make_async_copy(
                acc.at[pl.ds(r, SUBM)], ys_ref.at[pl.ds(row0 + r, SUBM)], sem_out)

        def start(i, c):
            out_copy(i).start()
            return c
        lax.fori_loop(0, n, start, 0)

        def wait(i, c):
            out_copy(i).wait()
            return c
        lax.fori_loop(0, n, wait, 0)


def _moe_grouped(xs, w1, w3, w2, st_e, st_row0, st_n, tail0):
    n_rows = xs.shape[0]
    n_st = st_e.shape[0]

    def fidx(g, f, ns):
        return jnp.where(ns[g] > 0, f, NF - 1)

    grid_spec = pltpu.PrefetchScalarGridSpec(
        num_scalar_prefetch=4,
        grid=(n_st, NF),
        in_specs=[
            pl.BlockSpec(memory_space=pl.ANY),
            pl.BlockSpec((None, D, TF), lambda g, f, se, r0, ns, tl: (se[g], 0, fidx(g, f, ns))),
            pl.BlockSpec((None, D, TF), lambda g, f, se, r0, ns, tl: (se[g], 0, fidx(g, f, ns))),
            pl.BlockSpec((None, TF, D), lambda g, f, se, r0, ns, tl: (se[g], fidx(g, f, ns), 0)),
        ],
        out_specs=pl.BlockSpec(memory_space=pl.ANY),
        scratch_shapes=[pltpu.VMEM((SPT * SUBM, D), F32), pltpu.VMEM((SPT * SUBM, D), F32),
                        pltpu.SemaphoreType.DMA(()), pltpu.SemaphoreType.DMA(())],
    )
    return pl.pallas_call(
        functools.partial(_moe_body, n_rows=n_rows),
        out_shape=jax.ShapeDtypeStruct((n_rows, D), F32),
        grid_spec=grid_spec,
        compiler_params=pltpu.CompilerParams(
            dimension_semantics=("arbitrary", "arbitrary"), vmem_limit_bytes=VMEM_LIMIT,
            has_side_effects=True),
        name="moe_grouped",
    )(st_e, st_row0, st_n, tail0, xs, w1, w3, w2)


def _combine_body(off_ref, ri_ref, rg_ref, x_ref, mod_ref, fg_ref, ys_ref, o_ref, ybuf, sem, *,
                  tc, reps, final):
    def row_copy(k, t, src_row):
        return pltpu.make_async_copy(ys_ref.at[pl.ds(src_row, 1)], ybuf.at[k, pl.ds(t, 1)], sem)

    def issue(t, c):
        row_copy(0, t, off_ref[ri_ref[0, t]] + ri_ref[2, t]).start()
        row_copy(1, t, off_ref[ri_ref[1, t]] + ri_ref[3, t]).start()
        return c
    lax.fori_loop(0, tc, issue, 0)

    def drain(t, c):
        row_copy(0, 0, 0).wait()
        return c
    lax.fori_loop(0, 2 * tc, drain, 0)

    g1 = jnp.transpose(jnp.broadcast_to(rg_ref[0:1, :], (LANE, tc)))
    g2 = jnp.transpose(jnp.broadcast_to(rg_ref[1:2, :], (LANE, tc)))
    g1 = jnp.concatenate([g1] * (D // LANE), axis=1)
    g2 = jnp.concatenate([g2] * (D // LANE), axis=1)
    x2 = x_ref[...] + _mod_chunk(mod_ref, 5, reps) * (g1 * ybuf[0] + g2 * ybuf[1])
    o_ref[...] = _rmsn(x2) * fg_ref[...] if final else x2


def _combine(x1, mod, final_g, ys, ri, rg, off, *, tc, tok_per_mod, final):
    n_tok = x1.shape[0]
    if tok_per_mod:
        mod_in = mod.reshape(mod.shape[0], 1, 6 * D)
        mod_spec = pl.BlockSpec((None, 1, 6 * D), lambda i, *_: ((i * tc) // tok_per_mod, 0, 0))
        reps = 1
    else:
        mod_in = mod
        mod_spec = pl.BlockSpec(mod.shape, lambda i, *_: (0, 0))
        reps = tc // mod.shape[0]
    grid_spec = pltpu.PrefetchScalarGridSpec(
        num_scalar_prefetch=1,
        grid=(n_tok // tc,),
        in_specs=[pl.BlockSpec((N_EXP, tc), lambda i, *_: (0, i), memory_space=pltpu.SMEM),
                  pl.BlockSpec((N_EXP, tc), lambda i, *_: (0, i)),
                  pl.BlockSpec((tc, D), lambda i, *_: (i, 0)),
                  mod_spec,
                  pl.BlockSpec((1, D), lambda i, *_: (0, 0)),
                  pl.BlockSpec(memory_space=pl.ANY)],
        out_specs=pl.BlockSpec((tc, D), lambda i, *_: (i, 0)),
        scratch_shapes=[pltpu.VMEM((2, tc, D), F32), pltpu.SemaphoreType.DMA(())],
    )
    return pl.pallas_call(
        functools.partial(_combine_body, tc=tc, reps=reps, final=final),
        out_shape=jax.ShapeDtypeStruct((n_tok, D), F32),
        grid_spec=grid_spec,
        compiler_params=pltpu.CompilerParams(
            dimension_semantics=("arbitrary",), vmem_limit_bytes=VMEM_LIMIT),
        name="moe_combine",
    )(off, ri, rg, x1, mod_in, final_g, ys)


def _final_norm_body(x_ref, fg_ref, o_ref):
    o_ref[...] = _rmsn(x_ref[...]) * fg_ref[...]


def _final_norm(x, final_g, *, tm=512):
    n_tok = x.shape[0]
    return pl.pallas_call(
        _final_norm_body,
        out_shape=jax.ShapeDtypeStruct((n_tok, D), F32),
        grid=(n_tok // tm,),
        in_specs=[pl.BlockSpec((tm, D), lambda i: (i, 0)), _const_spec((1, D))],
        out_specs=pl.BlockSpec((tm, D), lambda i: (i, 0)),
        name="final_norm",
    )(x, final_g)


def _expert_plan(counts, n_assign):
    cnt = counts[:, 0].astype(I32)
    nsub = (cnt + SUBM - 1) // SUBM
    off = (jnp.cumsum(nsub) - nsub) * SUBM
    pad0 = off + cnt
    padn = nsub * SUBM - cnt
    nst = (nsub + SPT - 1) // SPT
    max_sub = n_assign // SUBM + N_EXP
    n_st = (max_sub + N_EXP * (SPT - 1)) // SPT
    st_end = jnp.cumsum(nst)
    g = jnp.arange(n_st, dtype=I32)
    total = st_end[-1]
    valid = g < total
    owner = lambda q: jnp.minimum(jnp.sum((st_end[None, :] <= q[:, None]).astype(I32), axis=1), N_EXP - 1)
    e = jnp.where(valid, owner(g), owner((total - 1).reshape(1)))
    k = g - (st_end[e] - nst[e])
    base = nsub[e] // jnp.maximum(nst[e], 1)
    rem = nsub[e] - base * nst[e]
    size = jnp.where(valid, base + (k < rem).astype(I32), 0)
    sub0 = k * base + jnp.minimum(k, rem)
    row0 = jnp.where(valid, off[e] + sub0 * SUBM, 0)
    n_rows = max_sub * SUBM + SPT * SUBM
    tail0 = (jnp.sum(nsub) * SUBM).reshape(1)
    as_i32 = lambda *xs: tuple(x.astype(I32) for x in xs)
    return as_i32(off, pad0, padn, tail0, e, row0, size) + (n_rows,)


def _layer_weights(l, sample, norm1_g, norm2_g, w_in, v_norm_g, v_norm_b, sgu_w, sgu_b,
                   pool_w, pool_scale, branch_a_g, branch_b_g, w_out, n_pos):
    row = lambda a: a[l].reshape(1, -1)
    if sample:
        w8 = jnp.transpose(sgu_w[l][:, :n_pos, :n_pos], (1, 2, 0))
        sw = jnp.repeat(w8, HEAD_D, axis=2).reshape(n_pos * n_pos, SGU_W)
        sb = jnp.repeat(sgu_b[l][:, :n_pos].T, HEAD_D, axis=1)
    else:
        sw = sgu_w[l]
        sb = jnp.repeat(sgu_b[l].T, HEAD_D, axis=1)
    pw = pool_w[l].astype(BF16)
    z = jnp.zeros((LANE, LANE), BF16)
    pw2 = jnp.stack([jnp.block([[pw[0], z], [z, pw[1]]]), jnp.block([[pw[2], z], [z, pw[3]]])])
    return [row(norm1_g), row(norm2_g), w_in[l].astype(BF16), row(v_norm_g), row(v_norm_b), sw, sb,
            pw2, row(pool_scale), row(branch_a_g), row(branch_b_g), w_out[l].astype(BF16)]


def kernel(x_prompt, x_sample, state_pool, c_prompt, c_sample, norm1_g, norm2_g, ada_w, ada_b, w_in, v_norm_g, v_norm_b, sgu_w, sgu_b, pool_w, pool_scale, branch_a_g, branch_b_g, w_out, ffn_w1, ffn_w3, ffn_w2, router_w, moe_w1, moe_w3, moe_w2, final_g):
    n_b, seq, _ = x_prompt.shape
    n_seq, n_pos, _ = x_sample.shape
    depth = ada_w.shape[0]
    n_p = n_b * seq
    n_s = n_seq * n_pos
    n_all = n_p + n_s

    mod = _adaln(jnp.concatenate([c_prompt, c_sample], axis=0), ada_w, ada_b)
    mod_p, mod_s = mod[:, :n_b], mod[:, n_b:]

    xp = x_prompt.reshape(n_p, D)
    xs = jnp.swapaxes(x_sample, 0, 1).reshape(n_s, D)
    bufs = jnp.transpose(state_pool, (0, 2, 1, 3))
    fg = final_g.reshape(1, D)
    per_layer = (norm1_g, norm2_g, w_in, v_norm_g, v_norm_b, sgu_w, sgu_b, pool_w, pool_scale,
                 branch_a_g, branch_b_g, w_out)

    pool_p, pool_s, cv_p, cv_s = [], [], [], []
    normed = False
    for l in range(depth):
        lw_p = _layer_weights(l, False, *per_layer, n_pos)
        lw_s = _layer_weights(l, True, *per_layer, n_pos)
        g2 = norm2_g[l].reshape(1, D)
        if l % 2 == 0:
            x1p, pp, vp = _mixer_prompt(xp, mod_p[l], lw_p)
            x1s, ps, vs = _mixer_sample(xs, mod_s[l], bufs[l], lw_s)
            w1, w3, w2 = (w[l // 2].astype(BF16) for w in (ffn_w1, ffn_w3, ffn_w2))
            xp = _ffn(x1p, mod_p[l], g2, w1, w3, w2, tm=512, tok_per_mod=seq)
            xs = _ffn(x1s, mod_s[l], g2, w1, w3, w2, tm=2 * n_seq, tok_per_mod=0)
            normed = False
        else:
            m = l // 2
            rw = jnp.zeros((2 * N_EXP, D), BF16).at[:N_EXP].set(router_w[m].T.astype(BF16))
            cnt0 = jnp.zeros((N_EXP, LANE), F32)
            x1p, h2_p, ri_p, rg_p, cnt1, pp, vp = _mixer_prompt(
                xp, mod_p[l], lw_p, route_w=rw, cnt_in=cnt0)
            x1s, h2_s, ri_s, rg_s, cnt2, ps, vs = _mixer_sample(
                xs, mod_s[l], bufs[l], lw_s, route_w=rw, cnt_in=cnt1)
            off, pad0, padn, tail0, st_e, st_row0, st_n, n_rows = _expert_plan(cnt2, 2 * n_all)
            x_sorted = _scatter_rows(h2_p, h2_s, ri_p, ri_s, off, pad0, padn, tail0, n_rows)
            y_sorted = _moe_grouped(x_sorted, moe_w1[m], moe_w3[m], moe_w2[m],
                                    st_e, st_row0, st_n, tail0)
            normed = l == depth - 1
            xp = _combine(x1p, mod_p[l], fg, y_sorted, ri_p, rg_p, off,
                          tc=256, tok_per_mod=seq, final=normed)
            xs = _combine(x1s, mod_s[l], fg, y_sorted, ri_s, rg_s, off,
                          tc=2 * n_seq, tok_per_mod=0, final=normed)
        pool_p.append(pp[:, HALO - POOL_BUF:])
        pool_s.append(jnp.swapaxes(ps, 0, 1))
        cv_p.append(vp)
        cv_s.append(jnp.swapaxes(vs, 0, 1))

    if not normed:
        xp = _final_norm(xp, fg)
        xs = _final_norm(xs, fg)
    y_prompt = xp.reshape(n_b, seq, D)
    y_sample = jnp.swapaxes(xs.reshape(n_pos, n_seq, D), 0, 1)
    return (y_prompt, y_sample, jnp.stack(pool_p), jnp.stack(pool_s), jnp.stack(cv_p), jnp.stack(cv_s))
```

```python
import functools

import jax
import jax.numpy as jnp
from jax import lax
from jax.experimental import pallas as pl
from jax.experimental.pallas import tpu as pltpu

F32 = jnp.float32
BF16 = jnp.bfloat16
I32 = jnp.int32

D = 1024
SGU_W = 512
POOL_W = 512
HEADS = 4
HEAD_D = 128
CHUNK = 128
WINDOWS = (2, 4, 8, 16)
POOL_BUF = 15
HALO = 16
IN_W = 2 * SGU_W + POOL_W
D_FF = 3584
N_EXP = 8
TOP_K = 2
EPS = 1e-6
PAST_LEN = 16384

LANE = 128
SUBLANE = 8
TF = 512
NF = D_FF // TF
TR = 512
UNIT = SUBLANE
LR = -(-(TOP_K * TR + N_EXP * (UNIT - 1)) // LANE) * LANE
SUBM = 256
SPT = 8
VMEM_LIMIT = 56 * 1024 * 1024


def _rmsn(x):
    return x * lax.rsqrt(jnp.mean(x * x, axis=-1, keepdims=True) + EPS)


def _expand(m, reps):
    return m if reps == 1 else jnp.concatenate([m] * reps, axis=0)


def _mod_chunk(mod_ref, i, reps):
    return _expand(mod_ref[:, i * D:(i + 1) * D], reps)


def _dot(a, b):
    return jnp.dot(a, b, preferred_element_type=F32)


def _adaln_body(c_ref, w_ref, b_ref, o_ref):
    s = jax.nn.silu(c_ref[...]).astype(BF16)
    o_ref[...] = _dot(s, w_ref[...].astype(BF16)) + b_ref[...]


def _adaln(c_all, ada_w, ada_b):
    depth, _, n_out = ada_w.shape
    rows = c_all.shape[0]
    tn = 1024
    return pl.pallas_call(
        _adaln_body,
        out_shape=jax.ShapeDtypeStruct((depth, rows, n_out), F32),
        grid=(depth, n_out // tn),
        in_specs=[
            pl.BlockSpec((rows, D), lambda l, n: (0, 0)),
            pl.BlockSpec((None, D, tn), lambda l, n: (l, 0, n)),
            pl.BlockSpec((None, 1, tn), lambda l, n: (l, 0, n)),
        ],
        out_specs=pl.BlockSpec((None, rows, tn), lambda l, n: (l, 0, n)),
        compiler_params=pltpu.CompilerParams(
            dimension_semantics=("arbitrary", "arbitrary"), vmem_limit_bytes=VMEM_LIMIT),
        name="adaln",
    )(c_all, ada_w, ada_b.reshape(depth, 1, n_out))


def _mixer_front(x, mod_ref, g1_ref, win_ref, vg_ref, vb_ref, reps):
    sh1 = _mod_chunk(mod_ref, 0, reps)
    sc1 = _mod_chunk(mod_ref, 1, reps)
    h = _rmsn(x) * g1_ref[...] * (1.0 + sc1) + sh1
    z = _dot(h.astype(BF16), win_ref[...])
    u = jax.nn.gelu(z[:, :SGU_W])
    vr = jax.nn.gelu(z[:, SGU_W:2 * SGU_W])
    p = z[:, 2 * SGU_W:]
    vs = []
    for hh in range(HEADS):
        vh = vr[:, hh * HEAD_D:(hh + 1) * HEAD_D]
        dlt = vh - jnp.mean(vh, axis=-1, keepdims=True)
        var = jnp.mean(dlt * dlt, axis=-1, keepdims=True)
        vs.append(dlt * lax.rsqrt(var + EPS))
    v = jnp.concatenate(vs, axis=-1) * vg_ref[...] + vb_ref[...]
    return u, v, p


def _mixer_back(x, ya, d, mod_ref, g2_ref, pw_ref, ps_ref, ga_ref, gb_ref, wout_ref, reps):
    db = d.astype(BF16)
    yb = jnp.concatenate(
        [_dot(db[:, :2 * LANE], pw_ref[0]), _dot(db[:, 2 * LANE:], pw_ref[1])], axis=-1) * ps_ref[...]
    mixin = jnp.concatenate([_rmsn(ya) * ga_ref[...], _rmsn(yb) * gb_ref[...]], axis=-1)
    mix = _dot(mixin.astype(BF16), wout_ref[...])
    x1 = x + _mod_chunk(mod_ref, 2, reps) * mix
    sh2 = _mod_chunk(mod_ref, 3, reps)
    sc2 = _mod_chunk(mod_ref, 4, reps)
    h2 = _rmsn(x1) * g2_ref[...] * (1.0 + sc2) + sh2
    return x1, h2


def _route_sort(h2, rw_ref, u_ref):
    t = h2.shape[0]
    hb = h2.astype(BF16)
    logits = lax.dot_general(rw_ref[...], hb, (((1,), (1,)), ((), ())),
                             preferred_element_type=F32)[:N_EXP]
    sub = lax.broadcasted_iota(I32, (N_EXP, t), 0).astype(F32)
    m1 = jnp.max(logits, axis=0, keepdims=True)
    i1 = jnp.min(jnp.where(logits == m1, sub, float(N_EXP)), axis=0, keepdims=True)
    rest = jnp.where(sub == i1, -jnp.inf, logits)
    m2 = jnp.max(rest, axis=0, keepdims=True)
    i2 = jnp.min(jnp.where(rest == m2, sub, float(N_EXP)), axis=0, keepdims=True)
    e2 = jnp.exp(m2 - m1)
    den = 1.0 + e2
    g1 = 1.0 / den
    g2 = e2 / den
    sub16 = lax.broadcasted_iota(I32, (2 * N_EXP, t), 0).astype(F32)
    mask16 = jnp.where((sub16 == i1) | (sub16 == i2), 1.0, 0.0)
    rank = _dot(mask16.astype(BF16), u_ref[...])[:N_EXP]
    cnt = jnp.sum(mask16[:N_EXP], axis=1, keepdims=True)
    cnt_pad = jnp.floor((cnt + float(UNIT - 1)) * (1.0 / UNIT)) * float(UNIT)
    sub1 = sub[:, 0:1]
    lo = jnp.zeros((N_EXP, 1), F32)
    for e in range(N_EXP - 1):
        lo = lo + jnp.where(sub1 > float(e), cnt_pad[e:e + 1, :], 0.0)
    base = lo + rank
    ls1 = jnp.sum(jnp.where(sub == i1, base, 0.0), axis=0, keepdims=True)
    ls2 = jnp.sum(jnp.where(sub == i2, base, 0.0), axis=0, keepdims=True)
    srow = lax.broadcasted_iota(I32, (LR, t), 0).astype(F32)
    perm = jnp.where((srow == ls1) | (srow == ls2), 1.0, 0.0).astype(BF16)
    local = _dot(perm, hb)
    zf = jnp.zeros((N_EXP, t), F32)
    route = jnp.where(sub == 0.0, ls1, jnp.where(sub == 1.0, ls2, jnp.where(
        sub == 2.0, g1, jnp.where(sub == 3.0, g2, zf))))
    return local, route, cnt_pad, lo


def _mixer_prompt_body(*refs, T, n_tiles, emit_route):
    (x_ref, mod_ref, g1_ref, g2_ref, win_ref, vg_ref, vb_ref, sw_ref, sb_ref,
     pw_ref, ps_ref, ga_ref, gb_ref, wout_ref) = refs[:14]
    k = 14
    if emit_route:
        rw_ref, u_ref = refs[k:k + 2]
        k += 2
    x1_ref = refs[k]
    k += 1
    if emit_route:
        l_ref, rt_ref, segn_ref, seg0_ref = refs[k:k + 4]
        k += 4
    pstate_ref, cv_ref, pbuf = refs[k:k + 3]

    j = pl.program_id(1)
    x = x_ref[...]
    u, v, p = _mixer_front(x, mod_ref, g1_ref, win_ref, vg_ref, vb_ref, 1)

    nc = T // CHUNK
    row = lax.broadcasted_iota(I32, (CHUNK, CHUNK), 0)
    col = lax.broadcasted_iota(I32, (CHUNK, CHUNK), 1)
    vb16 = v.astype(BF16)
    heads = []
    for hh in range(HEADS):
        w = jnp.where(row >= col, sw_ref[hh], 0.0).astype(BF16)
        vcat = jnp.concatenate(
            [vb16[c * CHUNK:(c + 1) * CHUNK, hh * HEAD_D:(hh + 1) * HEAD_D] for c in range(nc)], axis=1)
        heads.append(_dot(w, vcat))
    mixed = jnp.concatenate(
        [jnp.concatenate([heads[hh][:, c * HEAD_D:(c + 1) * HEAD_D] for hh in range(HEADS)], axis=1)
         for c in range(nc)], axis=0)
    ya = u * (mixed + _expand(sb_ref[...], nc))

    @pl.when(j == 0)
    def _():
        pbuf[0:HALO, :] = jnp.zeros((HALO, POOL_W), F32)

    pbuf[HALO:HALO + T, :] = p
    pos = j * T + lax.broadcasted_iota(I32, (T, 1), 0)
    ds = []
    for g, win in enumerate(WINDOWS):
        lo = g * LANE
        ws = p[:, lo:lo + LANE]
        for kk in range(1, win):
            ws = ws + pbuf[HALO - kk:HALO - kk + T, lo:lo + LANE]
        cnt = jnp.minimum(win, pos + 1).astype(F32)
        ds.append(ws / cnt - p[:, lo:lo + LANE])
    d = jnp.concatenate(ds, axis=-1)
    pbuf[0:HALO, :] = p[T - HALO:, :]

    x1, h2 = _mixer_back(x, ya, d, mod_ref, g2_ref, pw_ref, ps_ref, ga_ref, gb_ref, wout_ref, 1)
    x1_ref[...] = x1

    @pl.when(j == n_tiles - 1)
    def _():
        pstate_ref[...] = p[T - HALO:, :]
        cv_ref[...] = v[T - CHUNK:, :]

    if emit_route:
        local, route, seg_n, seg_0 = _route_sort(h2, rw_ref, u_ref)
        l_ref[...] = local
        rt_ref[...] = route
        segn_ref[...] = jnp.broadcast_to(seg_n, (N_EXP, LANE))
        seg0_ref[...] = jnp.broadcast_to(seg_0, (N_EXP, LANE))


def _mixer_sample_body(*refs, n_seq, n_pos, emit_route):
    (x_ref, mod_ref, buf_ref, g1_ref, g2_ref, win_ref, vg_ref, vb_ref, sw_ref, sb_ref,
     pw_ref, ps_ref, ga_ref, gb_ref, wout_ref) = refs[:15]
    k = 15
    if emit_route:
        rw_ref, u_ref = refs[k:k + 2]
        k += 2
    x1_ref = refs[k]
    k += 1
    if emit_route:
        l_ref, rt_ref, segn_ref, seg0_ref = refs[k:k + 4]
        k += 4
    pstate_ref, cv_ref = refs[k:k + 2]

    reps = n_pos
    x = x_ref[...]
    u, v, p = _mixer_front(x, mod_ref, g1_ref, win_ref, vg_ref, vb_ref, reps)

    def slab(a, t):
        return a[t * n_seq:(t + 1) * n_seq, :]

    mixed = []
    for t in range(n_pos):
        acc = sb_ref[t:t + 1, :] + sw_ref[t * n_pos:t * n_pos + 1, :] * slab(v, 0)
        for s in range(1, t + 1):
            acc = acc + sw_ref[t * n_pos + s:t * n_pos + s + 1, :] * slab(v, s)
        mixed.append(acc)
    ya = u * jnp.concatenate(mixed, axis=0)

    hist = [buf_ref[jj] for jj in range(POOL_BUF)] + [slab(p, t) for t in range(n_pos)]
    drows = []
    for t in range(n_pos):
        parts = []
        for g, win in enumerate(WINDOWS):
            lo = g * LANE
            ws = hist[POOL_BUF + t][:, lo:lo + LANE]
            for kk in range(1, win):
                ws = ws + hist[POOL_BUF + t - kk][:, lo:lo + LANE]
            cnt = float(min(win, PAST_LEN + t + 1))
            parts.append(ws / cnt - hist[POOL_BUF + t][:, lo:lo + LANE])
        drows.append(jnp.concatenate(parts, axis=-1))
    d = jnp.concatenate(drows, axis=0)
    for jj in range(POOL_BUF):
        pstate_ref[jj] = hist[n_pos + jj]
    for t in range(n_pos):
        cv_ref[t] = slab(v, t)

    x1, h2 = _mixer_back(x, ya, d, mod_ref, g2_ref, pw_ref, ps_ref, ga_ref, gb_ref, wout_ref, reps)
    x1_ref[...] = x1

    if emit_route:
        n_rt = (n_seq * n_pos) // TR
        for i in range(n_rt):
            local, route, seg_n, seg_0 = _route_sort(h2[i * TR:(i + 1) * TR, :], rw_ref, u_ref)
            l_ref[i * LR:(i + 1) * LR, :] = local
            rt_ref[:, i * TR:(i + 1) * TR] = route
            segn_ref[i] = jnp.broadcast_to(seg_n, (N_EXP, LANE))
            seg0_ref[i] = jnp.broadcast_to(seg_0, (N_EXP, LANE))
        l_ref[n_rt * LR:n_rt * LR + UNIT, :] = jnp.zeros((UNIT, D), F32)


def _const_spec(shape):
    nd = len(shape)
    return pl.BlockSpec(shape, lambda *_: (0,) * nd)


def _rank_matrix():
    return jnp.triu(jnp.ones((TR, TR), BF16), 1)


def _mixer_prompt(x, mod, lw, route_w=None):
    T = TR
    n_b = mod.shape[0]
    n_tok = x.shape[0]
    seq = n_tok // n_b
    n_tiles = seq // T
    n_rt = n_tok // T
    emit_route = route_w is not None
    tile = lambda b, j: b * n_tiles + j
    in_specs = [
        pl.BlockSpec((T, D), lambda b, j: (tile(b, j), 0)),
        pl.BlockSpec((None, 1, 6 * D), lambda b, j: (b, 0, 0)),
    ] + [_const_spec(w.shape) for w in lw]
    args = [x, mod.reshape(n_b, 1, 6 * D)] + list(lw)
    out_shape = [jax.ShapeDtypeStruct((n_tok, D), F32)]
    out_specs = [pl.BlockSpec((T, D), lambda b, j: (tile(b, j), 0))]
    if emit_route:
        umat = _rank_matrix()
        in_specs += [_const_spec(route_w.shape), _const_spec(umat.shape)]
        args += [route_w, umat]
        out_shape += [jax.ShapeDtypeStruct((n_rt * LR, D), F32),
                      jax.ShapeDtypeStruct((N_EXP, n_tok), F32),
                      jax.ShapeDtypeStruct((n_rt, N_EXP, LANE), F32),
                      jax.ShapeDtypeStruct((n_rt, N_EXP, LANE), F32)]
        out_specs += [pl.BlockSpec((LR, D), lambda b, j: (tile(b, j), 0)),
                      pl.BlockSpec((N_EXP, T), lambda b, j: (0, tile(b, j))),
                      pl.BlockSpec((None, N_EXP, LANE), lambda b, j: (tile(b, j), 0, 0)),
                      pl.BlockSpec((None, N_EXP, LANE), lambda b, j: (tile(b, j), 0, 0))]
    out_shape += [jax.ShapeDtypeStruct((n_b, HALO, POOL_W), F32),
                  jax.ShapeDtypeStruct((n_b, CHUNK, SGU_W), F32)]
    out_specs += [pl.BlockSpec((None, HALO, POOL_W), lambda b, j: (b, 0, 0)),
                  pl.BlockSpec((None, CHUNK, SGU_W), lambda b, j: (b, 0, 0))]
    return pl.pallas_call(
        functools.partial(_mixer_prompt_body, T=T, n_tiles=n_tiles, emit_route=emit_route),
        out_shape=out_shape,
        grid=(n_b, n_tiles),
        in_specs=in_specs,
        out_specs=out_specs,
        scratch_shapes=[pltpu.VMEM((HALO + T, POOL_W), F32)],
        compiler_params=pltpu.CompilerParams(
            dimension_semantics=("arbitrary", "arbitrary"), vmem_limit_bytes=VMEM_LIMIT),
        name="mixer_prompt_route" if emit_route else "mixer_prompt",
    )(*args)


def _mixer_sample(x, mod, buf, lw, route_w=None):
    n_seq = mod.shape[0]
    n_tok = x.shape[0]
    n_pos = n_tok // n_seq
    n_rt = n_tok // TR
    emit_route = route_w is not None
    in_specs = [_const_spec(x.shape), _const_spec(mod.shape), _const_spec(buf.shape)]
    in_specs += [_const_spec(w.shape) for w in lw]
    args = [x, mod, buf] + list(lw)
    out_shape = [jax.ShapeDtypeStruct((n_tok, D), F32)]
    if emit_route:
        umat = _rank_matrix()
        in_specs += [_const_spec(route_w.shape), _const_spec(umat.shape)]
        args += [route_w, umat]
        out_shape += [jax.ShapeDtypeStruct((n_rt * LR + UNIT, D), F32),
                      jax.ShapeDtypeStruct((N_EXP, n_tok), F32),
                      jax.ShapeDtypeStruct((n_rt, N_EXP, LANE), F32),
                      jax.ShapeDtypeStruct((n_rt, N_EXP, LANE), F32)]
    out_shape += [jax.ShapeDtypeStruct((POOL_BUF, n_seq, POOL_W), F32),
                  jax.ShapeDtypeStruct((n_pos, n_seq, SGU_W), F32)]
    return pl.pallas_call(
        functools.partial(_mixer_sample_body, n_seq=n_seq, n_pos=n_pos, emit_route=emit_route),
        out_shape=out_shape,
        grid=(1,),
        in_specs=in_specs,
        out_specs=[_const_spec(s.shape) for s in out_shape],
        compiler_params=pltpu.CompilerParams(
            dimension_semantics=("arbitrary",), vmem_limit_bytes=VMEM_LIMIT),
        name="mixer_sample_route" if emit_route else "mixer_sample",
    )(*args)


def _ffn_body(x_ref, mod_ref, g2_ref, w1_ref, w3_ref, w2_ref, o_ref, *, reps):
    x1 = x_ref[...]
    sh2 = _mod_chunk(mod_ref, 3, reps)
    sc2 = _mod_chunk(mod_ref, 4, reps)
    h = (_rmsn(x1) * g2_ref[...] * (1.0 + sc2) + sh2).astype(BF16)
    acc = None
    for c in range(NF):
        a = _dot(h, w1_ref[:, c * TF:(c + 1) * TF])
        bb = _dot(h, w3_ref[:, c * TF:(c + 1) * TF])
        part = _dot((jax.nn.silu(a) * bb).astype(BF16), w2_ref[c * TF:(c + 1) * TF, :])
        acc = part if acc is None else acc + part
    o_ref[...] = x1 + _mod_chunk(mod_ref, 5, reps) * acc


def _mod_spec(mod, tile_rows, tok_per_mod):
    if tok_per_mod:
        spec = pl.BlockSpec((None, 1, 6 * D), lambda i, *_: ((i * tile_rows) // tok_per_mod, 0, 0))
        return mod.reshape(mod.shape[0], 1, 6 * D), spec, 1
    return mod, pl.BlockSpec(mod.shape, lambda i, *_: (0, 0)), tile_rows // mod.shape[0]


def _ffn(x1, mod, g2, w1, w3, w2, *, tm, tok_per_mod):
    n_tok = x1.shape[0]
    mod_in, mod_spec, reps = _mod_spec(mod, tm, tok_per_mod)
    resident = lambda s: pl.BlockSpec(s, lambda i: (0,) * len(s), pipeline_mode=pl.Buffered(1))
    return pl.pallas_call(
        functools.partial(_ffn_body, reps=reps),
        out_shape=jax.ShapeDtypeStruct((n_tok, D), F32),
        grid=(n_tok // tm,),
        in_specs=[pl.BlockSpec((tm, D), lambda i: (i, 0)), mod_spec, _const_spec(g2.shape),
                  resident(w1.shape), resident(w3.shape), resident(w2.shape)],
        out_specs=pl.BlockSpec((tm, D), lambda i: (i, 0)),
        compiler_params=pltpu.CompilerParams(
            dimension_semantics=("arbitrary",), vmem_limit_bytes=VMEM_LIMIT),
        name="ffn_dense",
    )(x1, mod_in, g2, w1, w3, w2)


def _moe_body(se_ref, row0_ref, nsub_ref, usrc_ref, tl0_ref, tln_ref,
              lp_ref, ls_ref, w1_ref, w3_ref, w2_ref, yp_ref, ys_ref,
              xbuf, acc, sem_in, sem_out, *, units_p, zero_unit, n_rt):
    g = pl.program_id(0)
    f = pl.program_id(1)
    n = nsub_ref[g]
    u0 = row0_ref[g] // UNIT
    n_units = n * (SUBM // UNIT)

    def rows(unit):
        return pl.ds(pl.multiple_of(unit * UNIT, UNIT), UNIT)

    def on_owner(unit, fn):
        @pl.when(unit < units_p)
        def _():
            fn(0, unit)

        @pl.when(unit >= units_p)
        def _():
            fn(1, unit - units_p)

    def y_copy(owner, unit, src):
        return pltpu.make_async_copy(src, (yp_ref, ys_ref)[owner].at[rows(unit)], sem_out)

    def x_copy(owner, unit, u):
        return pltpu.make_async_copy((lp_ref, ls_ref)[owner].at[rows(unit)], xbuf.at[rows(u)], sem_in)

    @pl.when((g == 0) & (f == 0))
    def _():
        acc[0:UNIT, :] = jnp.zeros((UNIT, D), F32)
        for phase in ("start", "wait"):
            def tile_tail(i, c):
                def one(k, c2):
                    on_owner(tl0_ref[i] + k,
                             lambda w, unit: getattr(y_copy(w, unit, acc.at[pl.ds(0, UNIT)]), phase)())
                    return c2
                lax.fori_loop(0, tln_ref[i], one, 0)
                return c
            lax.fori_loop(0, n_rt, tile_tail, 0)

    @pl.when((f == 0) & (n > 0))
    def _():
        def start(u, c):
            on_owner(usrc_ref[u0 + u], lambda w, unit: x_copy(w, unit, u).start())
            return c
        lax.fori_loop(0, n_units, start, 0)

        def wait(u, c):
            x_copy(0, u, u).wait()
            return c
        lax.fori_loop(0, n_units, wait, 0)

    def run(first):
        w1b = w1_ref[...].astype(BF16)
        w3b = w3_ref[...].astype(BF16)
        w2b = w2_ref[...].astype(BF16)

        def block(r, m):
            h = xbuf[pl.ds(r, m), :].astype(BF16)
            a = _dot(h, w1b)
            bb = _dot(h, w3b)
            part = _dot((jax.nn.silu(a) * bb).astype(BF16), w2b)
            if first:
                acc[pl.ds(r, m), :] = part
            else:
                acc[pl.ds(r, m), :] = acc[pl.ds(r, m), :] + part

        def pair(i, c):
            block(pl.multiple_of(i * 2 * SUBM, 2 * SUBM), 2 * SUBM)
            return c
        lax.fori_loop(0, n // 2, pair, 0)

        @pl.when(n % 2 == 1)
        def _():
            block(pl.multiple_of((n - 1) * SUBM, SUBM), SUBM)

    @pl.when((f == 0) & (n > 0))
    def _():
        run(True)

    @pl.when((f > 0) & (n > 0))
    def _():
        run(False)

    @pl.when((f == NF - 1) & (n > 0))
    def _():
        for phase in ("start", "wait"):
            def one(u, c):
                unit = usrc_ref[u0 + u]

                @pl.when(unit != zero_unit)
                def _():
                    on_owner(unit, lambda w, k: getattr(y_copy(w, k, acc.at[rows(u)]), phase)())
                return c
            lax.fori_loop(0, n_units, one, 0)


def _moe_grouped(l_p, l_s, w1, w3, w2, plan):
    st_e, st_row0, st_n, usrc, tl0, tln = plan
    n_st = st_e.shape[0]
    n_rt = tl0.shape[0]
    units_p = l_p.shape[0] // UNIT
    n_y_s = l_s.shape[0] - UNIT

    def fidx(g, f, ns):
        return jnp.where(ns[g] > 0, f, NF - 1)

    grid_spec = pltpu.PrefetchScalarGridSpec(
        num_scalar_prefetch=6,
        grid=(n_st, NF),
        in_specs=[
            pl.BlockSpec(memory_space=pl.ANY),
            pl.BlockSpec(memory_space=pl.ANY),
            pl.BlockSpec((None, D, TF), lambda g, f, se, r0, ns, *_: (se[g], 0, fidx(g, f, ns))),
            pl.BlockSpec((None, D, TF), lambda g, f, se, r0, ns, *_: (se[g], 0, fidx(g, f, ns))),
            pl.BlockSpec((None, TF, D), lambda g, f, se, r0, ns, *_: (se[g], fidx(g, f, ns), 0)),
        ],
        out_specs=[pl.BlockSpec(memory_space=pl.ANY), pl.BlockSpec(memory_space=pl.ANY)],
        scratch_shapes=[pltpu.VMEM((SPT * SUBM, D), F32), pltpu.VMEM((SPT * SUBM, D), F32),
                        pltpu.SemaphoreType.DMA(()), pltpu.SemaphoreType.DMA(())],
    )
    return pl.pallas_call(
        functools.partial(_moe_body, units_p=units_p, zero_unit=(l_p.shape[0] + n_y_s) // UNIT, n_rt=n_rt),
        out_shape=[jax.ShapeDtypeStruct(l_p.shape, F32), jax.ShapeDtypeStruct((n_y_s, D), F32)],
        grid_spec=grid_spec,
        compiler_params=pltpu.CompilerParams(
            dimension_semantics=("arbitrary", "arbitrary"), vmem_limit_bytes=VMEM_LIMIT,
            has_side_effects=True),
        name="moe_grouped",
    )(st_e, st_row0, st_n, usrc, tl0, tln, l_p, l_s, w1, w3, w2)


def _combine_body(rt_ref, x_ref, mod_ref, fg_ref, y_ref, o_ref, *, reps, final):
    tc = x_ref.shape[0]

    def column(k):
        return jnp.transpose(jnp.broadcast_to(rt_ref[k:k + 1, :], (LANE, tc)))

    lane_id = lax.broadcasted_iota(I32, (tc, LR), 1).astype(F32)
    yb = y_ref[...].astype(BF16)

    def pick(k):
        slot = jnp.concatenate([column(k)] * (LR // LANE), axis=1)
        onehot = jnp.where(lane_id == slot, 1.0, 0.0).astype(BF16)
        return _dot(onehot, yb)

    g1 = jnp.concatenate([column(2)] * (D // LANE), axis=1)
    g2 = jnp.concatenate([column(3)] * (D // LANE), axis=1)
    x2 = x_ref[...] + _mod_chunk(mod_ref, 5, reps) * (g1 * pick(0) + g2 * pick(1))
    o_ref[...] = _rmsn(x2) * fg_ref[...] if final else x2


def _combine(x1, mod, final_g, y_local, route, *, tok_per_mod, final):
    n_tok = x1.shape[0]
    tc = TR
    mod_in, mod_spec, reps = _mod_spec(mod, tc, tok_per_mod)
    return pl.pallas_call(
        functools.partial(_combine_body, reps=reps, final=final),
        out_shape=jax.ShapeDtypeStruct((n_tok, D), F32),
        grid=(n_tok // tc,),
        in_specs=[pl.BlockSpec((N_EXP, tc), lambda i: (0, i)),
                  pl.BlockSpec((tc, D), lambda i: (i, 0)),
                  mod_spec,
                  _const_spec((1, D)),
                  pl.BlockSpec((LR, D), lambda i: (i, 0))],
        out_specs=pl.BlockSpec((tc, D), lambda i: (i, 0)),
        compiler_params=pltpu.CompilerParams(
            dimension_semantics=("arbitrary",), vmem_limit_bytes=VMEM_LIMIT),
        name="moe_combine",
    )(route, x1, mod_in, final_g, y_local)


def _final_norm_body(x_ref, fg_ref, o_ref):
    o_ref[...] = _rmsn(x_ref[...]) * fg_ref[...]


def _final_norm(x, final_g, *, tm=512):
    n_tok = x.shape[0]
    return pl.pallas_call(
        _final_norm_body,
        out_shape=jax.ShapeDtypeStruct((n_tok, D), F32),
        grid=(n_tok // tm,),
        in_specs=[pl.BlockSpec((tm, D), lambda i: (i, 0)), _const_spec((1, D))],
        out_specs=pl.BlockSpec((tm, D), lambda i: (i, 0)),
        name="final_norm",
    )(x, final_g)


def _expert_plan(seg_n, seg_0):
    n_rt = seg_n.shape[0]
    seg_n = seg_n.astype(I32)
    seg_0 = seg_0.astype(I32)
    total = jnp.sum(seg_n, axis=0)
    nsub = (total + SUBM - 1) // SUBM
    off = (jnp.cumsum(nsub) - nsub) * SUBM
    dst = off[None, :] + jnp.cumsum(seg_n, axis=0) - seg_n
    src = jnp.arange(n_rt, dtype=I32)[:, None] * LR + seg_0
    s_start = dst.T.reshape(-1) // UNIT
    s_len = seg_n.T.reshape(-1) // UNIT
    s_src = src.T.reshape(-1) // UNIT
    max_sub = (n_rt * (TOP_K * TR + N_EXP * (UNIT - 1))) // SUBM + N_EXP
    zero_unit = n_rt * LR // UNIT
    unit = jnp.arange(max_sub * SUBM // UNIT, dtype=I32)
    owner = jnp.sum((s_start[None, :] <= unit[:, None]).astype(I32), axis=1) - 1
    owner = jnp.clip(owner, 0, s_start.shape[0] - 1)
    k = unit - s_start[owner]
    usrc = jnp.where((k >= 0) & (k < s_len[owner]), s_src[owner] + k, zero_unit)
    nst = (nsub + SPT - 1) // SPT
    n_st = (max_sub + N_EXP * (SPT - 1)) // SPT
    st_end = jnp.cumsum(nst)
    g = jnp.arange(n_st, dtype=I32)
    n_used = st_end[-1]
    valid = g < n_used
    which = lambda q: jnp.minimum(jnp.sum((st_end[None, :] <= q[:, None]).astype(I32), axis=1), N_EXP - 1)
    e = jnp.where(valid, which(g), which((n_used - 1).reshape(1)))
    kk = g - (st_end[e] - nst[e])
    base = nsub[e] // jnp.maximum(nst[e], 1)
    rem = nsub[e] - base * nst[e]
    size = jnp.where(valid, base + (kk < rem).astype(I32), 0)
    row0 = jnp.where(valid, off[e] + (kk * base + jnp.minimum(kk, rem)) * SUBM, 0)
    used = jnp.sum(seg_n, axis=1)
    tail0 = (jnp.arange(n_rt, dtype=I32) * LR + used) // UNIT
    tailn = (LR - used) // UNIT
    as_i32 = lambda *xs: tuple(x.astype(I32) for x in xs)
    return as_i32(e, row0, size, usrc, tail0, tailn)


def _layer_weights(l, sample, norm1_g, norm2_g, w_in, v_norm_g, v_norm_b, sgu_w, sgu_b,
                   pool_w, pool_scale, branch_a_g, branch_b_g, w_out, n_pos):
    row = lambda a: a[l].reshape(1, -1)
    if sample:
        w8 = jnp.transpose(sgu_w[l][:, :n_pos, :n_pos], (1, 2, 0))
        sw = jnp.repeat(w8, HEAD_D, axis=2).reshape(n_pos * n_pos, SGU_W)
        sb = jnp.repeat(sgu_b[l][:, :n_pos].T, HEAD_D, axis=1)
    else:
        sw = sgu_w[l]
        sb = jnp.repeat(sgu_b[l].T, HEAD_D, axis=1)
    pw = pool_w[l].astype(BF16)
    z = jnp.zeros((LANE, LANE), BF16)
    pw2 = jnp.stack([jnp.block([[pw[0], z], [z, pw[1]]]), jnp.block([[pw[2], z], [z, pw[3]]])])
    return [row(norm1_g), row(norm2_g), w_in[l].astype(BF16), row(v_norm_g), row(v_norm_b), sw, sb,
            pw2, row(pool_scale), row(branch_a_g), row(branch_b_g), w_out[l].astype(BF16)]


def kernel(x_prompt, x_sample, state_pool, c_prompt, c_sample, norm1_g, norm2_g, ada_w, ada_b, w_in, v_norm_g, v_norm_b, sgu_w, sgu_b, pool_w, pool_scale, branch_a_g, branch_b_g, w_out, ffn_w1, ffn_w3, ffn_w2, router_w, moe_w1, moe_w3, moe_w2, final_g):
    n_b, seq, _ = x_prompt.shape
    n_seq, n_pos, _ = x_sample.shape
    depth = ada_w.shape[0]
    n_p = n_b * seq
    n_s = n_seq * n_pos

    mod = _adaln(jnp.concatenate([c_prompt, c_sample], axis=0), ada_w, ada_b)
    mod_p, mod_s = mod[:, :n_b], mod[:, n_b:]

    xp = x_prompt.reshape(n_p, D)
    xs = jnp.swapaxes(x_sample, 0, 1).reshape(n_s, D)
    bufs = jnp.transpose(state_pool, (0, 2, 1, 3))
    fg = final_g.reshape(1, D)
    per_layer = (norm1_g, norm2_g, w_in, v_norm_g, v_norm_b, sgu_w, sgu_b, pool_w, pool_scale,
                 branch_a_g, branch_b_g, w_out)

    pool_p, pool_s, cv_p, cv_s = [], [], [], []
    normed = False
    for l in range(depth):
        lw_p = _layer_weights(l, False, *per_layer, n_pos)
        lw_s = _layer_weights(l, True, *per_layer, n_pos)
        g2 = norm2_g[l].reshape(1, D)
        if l % 2 == 0:
            x1p, pp, vp = _mixer_prompt(xp, mod_p[l], lw_p)
            x1s, ps, vs = _mixer_sample(xs, mod_s[l], bufs[l], lw_s)
            w1, w3, w2 = (w[l // 2].astype(BF16) for w in (ffn_w1, ffn_w3, ffn_w2))
            xp = _ffn(x1p, mod_p[l], g2, w1, w3, w2, tm=512, tok_per_mod=seq)
            xs = _ffn(x1s, mod_s[l], g2, w1, w3, w2, tm=2 * n_seq, tok_per_mod=0)
            normed = False
        else:
            m = l // 2
            rw = jnp.zeros((2 * N_EXP, D), BF16).at[:N_EXP].set(router_w[m].T.astype(BF16))
            x1p, l_p, rt_p, segn_p, seg0_p, pp, vp = _mixer_prompt(xp, mod_p[l], lw_p, route_w=rw)
            x1s, l_s, rt_s, segn_s, seg0_s, ps, vs = _mixer_sample(xs, mod_s[l], bufs[l], lw_s, route_w=rw)
            plan = _expert_plan(jnp.concatenate([segn_p[:, :, 0], segn_s[:, :, 0]], axis=0),
                                jnp.concatenate([seg0_p[:, :, 0], seg0_s[:, :, 0]], axis=0))
            y_p, y_s = _moe_grouped(l_p, l_s, moe_w1[m], moe_w3[m], moe_w2[m], plan)
            normed = l == depth - 1
            xp = _combine(x1p, mod_p[l], fg, y_p, rt_p, tok_per_mod=seq, final=normed)
            xs = _combine(x1s, mod_s[l], fg, y_s, rt_s, tok_per_mod=0, final=normed)
        pool_p.append(pp[:, HALO - POOL_BUF:])
        pool_s.append(jnp.swapaxes(ps, 0, 1))
        cv_p.append(vp)
        cv_s.append(jnp.swapaxes(vs, 0, 1))

    if not normed:
        xp = _final_norm(xp, fg)
        xs = _final_norm(xs, fg)
    y_prompt = xp.reshape(n_b, seq, D)
    y_sample = jnp.swapaxes(xs.reshape(n_pos, n_seq, D), 0, 1)
    return (y_prompt, y_sample, jnp.stack(pool_p), jnp.stack(pool_s), jnp.stack(cv_p), jnp.stack(cv_s))
```

```python
import functools

import jax
import jax.numpy as jnp
from jax import lax
from jax.experimental import pallas as pl
from jax.experimental.pallas import tpu as pltpu

F32 = jnp.float32
BF16 = jnp.bfloat16
I32 = jnp.int32

D = 1024
SGU_W = 512
POOL_W = 512
HEADS = 4
HEAD_D = 128
CHUNK = 128
WINDOWS = (2, 4, 8, 16)
POOL_BUF = 15
HALO = 16
IN_W = 2 * SGU_W + POOL_W
D_FF = 3584
N_EXP = 8
TOP_K = 2
EPS = 1e-6
PAST_LEN = 16384

LANE = 128
SUBLANE = 8
TF = 512
NF = D_FF // TF
TR = 512
UNIT = SUBLANE
LR = -(-(TOP_K * TR + N_EXP * (UNIT - 1)) // LANE) * LANE
SUBM = 256
SPT = 8
VMEM_LIMIT = 56 * 1024 * 1024


def _rmsn(x):
    return x * lax.rsqrt(jnp.mean(x * x, axis=-1, keepdims=True) + EPS)


def _expand(m, reps):
    return m if reps == 1 else jnp.concatenate([m] * reps, axis=0)


def _mod_chunk(mod_ref, i, reps):
    return _expand(mod_ref[:, i * D:(i + 1) * D], reps)


def _dot(a, b):
    return jnp.dot(a, b, preferred_element_type=F32)


def _adaln_body(c_ref, w_ref, b_ref, o_ref):
    s = jax.nn.silu(c_ref[...]).astype(BF16)
    o_ref[...] = _dot(s, w_ref[...].astype(BF16)) + b_ref[...]


def _adaln(c_all, ada_w, ada_b):
    depth, _, n_out = ada_w.shape
    rows = c_all.shape[0]
    tn = 1024
    return pl.pallas_call(
        _adaln_body,
        out_shape=jax.ShapeDtypeStruct((depth, rows, n_out), F32),
        grid=(depth, n_out // tn),
        in_specs=[
            pl.BlockSpec((rows, D), lambda l, n: (0, 0)),
            pl.BlockSpec((None, D, tn), lambda l, n: (l, 0, n)),
            pl.BlockSpec((None, 1, tn), lambda l, n: (l, 0, n)),
        ],
        out_specs=pl.BlockSpec((None, rows, tn), lambda l, n: (l, 0, n)),
        compiler_params=pltpu.CompilerParams(
            dimension_semantics=("arbitrary", "arbitrary"), vmem_limit_bytes=VMEM_LIMIT),
        name="adaln",
    )(c_all, ada_w, ada_b.reshape(depth, 1, n_out))


def _mixer_front(x, mod_ref, g1_ref, win_ref, vg_ref, vb_ref, reps):
    sh1 = _mod_chunk(mod_ref, 0, reps)
    sc1 = _mod_chunk(mod_ref, 1, reps)
    h = _rmsn(x) * g1_ref[...] * (1.0 + sc1) + sh1
    z = _dot(h.astype(BF16), win_ref[...])
    u = jax.nn.gelu(z[:, :SGU_W])
    vr = jax.nn.gelu(z[:, SGU_W:2 * SGU_W])
    p = z[:, 2 * SGU_W:]
    vs = []
    for hh in range(HEADS):
        vh = vr[:, hh * HEAD_D:(hh + 1) * HEAD_D]
        dlt = vh - jnp.mean(vh, axis=-1, keepdims=True)
        var = jnp.mean(dlt * dlt, axis=-1, keepdims=True)
        vs.append(dlt * lax.rsqrt(var + EPS))
    v = jnp.concatenate(vs, axis=-1) * vg_ref[...] + vb_ref[...]
    return u, v, p


def _mixer_back(x, ya, d, mod_ref, g2_ref, pw_ref, ps_ref, ga_ref, gb_ref, wout_ref, reps):
    db = d.astype(BF16)
    yb = jnp.concatenate(
        [_dot(db[:, :2 * LANE], pw_ref[0]), _dot(db[:, 2 * LANE:], pw_ref[1])], axis=-1) * ps_ref[...]
    mixin = jnp.concatenate([_rmsn(ya) * ga_ref[...], _rmsn(yb) * gb_ref[...]], axis=-1)
    mix = _dot(mixin.astype(BF16), wout_ref[...])
    x1 = x + _mod_chunk(mod_ref, 2, reps) * mix
    sh2 = _mod_chunk(mod_ref, 3, reps)
    sc2 = _mod_chunk(mod_ref, 4, reps)
    h2 = _rmsn(x1) * g2_ref[...] * (1.0 + sc2) + sh2
    return x1, h2


def _route_sort(h2, rw_ref, u_ref):
    t = h2.shape[0]
    hb = h2.astype(BF16)
    logits = lax.dot_general(rw_ref[...], hb, (((1,), (1,)), ((), ())),
                             preferred_element_type=F32)[:N_EXP]
    sub = lax.broadcasted_iota(I32, (N_EXP, t), 0).astype(F32)
    m1 = jnp.max(logits, axis=0, keepdims=True)
    i1 = jnp.min(jnp.where(logits == m1, sub, float(N_EXP)), axis=0, keepdims=True)
    rest = jnp.where(sub == i1, -jnp.inf, logits)
    m2 = jnp.max(rest, axis=0, keepdims=True)
    i2 = jnp.min(jnp.where(rest == m2, sub, float(N_EXP)), axis=0, keepdims=True)
    e2 = jnp.exp(m2 - m1)
    den = 1.0 + e2
    g1 = 1.0 / den
    g2 = e2 / den
    sub16 = lax.broadcasted_iota(I32, (2 * N_EXP, t), 0).astype(F32)
    mask16 = jnp.where((sub16 == i1) | (sub16 == i2), 1.0, 0.0)
    rank = _dot(mask16.astype(BF16), u_ref[...])[:N_EXP]
    cnt = jnp.sum(mask16[:N_EXP], axis=1, keepdims=True)
    cnt_pad = jnp.floor((cnt + float(UNIT - 1)) * (1.0 / UNIT)) * float(UNIT)
    sub1 = sub[:, 0:1]
    lo = jnp.zeros((N_EXP, 1), F32)
    for e in range(N_EXP - 1):
        lo = lo + jnp.where(sub1 > float(e), cnt_pad[e:e + 1, :], 0.0)
    base = lo + rank
    ls1 = jnp.sum(jnp.where(sub == i1, base, 0.0), axis=0, keepdims=True)
    ls2 = jnp.sum(jnp.where(sub == i2, base, 0.0), axis=0, keepdims=True)
    srow = lax.broadcasted_iota(I32, (LR, t), 0).astype(F32)
    perm = jnp.where((srow == ls1) | (srow == ls2), 1.0, 0.0).astype(BF16)
    local = _dot(perm, hb)
    zf = jnp.zeros((N_EXP, t), F32)
    route = jnp.where(sub == 0.0, ls1, jnp.where(sub == 1.0, ls2, jnp.where(
        sub == 2.0, g1, jnp.where(sub == 3.0, g2, zf))))
    return local, route, cnt_pad, lo


def _mixer_prompt_body(*refs, T, n_tiles, emit_route):
    (x_ref, mod_ref, g1_ref, g2_ref, win_ref, vg_ref, vb_ref, sw_ref, sb_ref,
     pw_ref, ps_ref, ga_ref, gb_ref, wout_ref) = refs[:14]
    k = 14
    if emit_route:
        rw_ref, u_ref = refs[k:k + 2]
        k += 2
    x1_ref = refs[k]
    k += 1
    if emit_route:
        l_ref, rt_ref, segn_ref, seg0_ref = refs[k:k + 4]
        k += 4
    pstate_ref, cv_ref, pbuf = refs[k:k + 3]

    j = pl.program_id(1)
    x = x_ref[...]
    u, v, p = _mixer_front(x, mod_ref, g1_ref, win_ref, vg_ref, vb_ref, 1)

    nc = T // CHUNK
    row = lax.broadcasted_iota(I32, (CHUNK, CHUNK), 0)
    col = lax.broadcasted_iota(I32, (CHUNK, CHUNK), 1)
    vb16 = v.astype(BF16)
    heads = []
    for hh in range(HEADS):
        w = jnp.where(row >= col, sw_ref[hh], 0.0).astype(BF16)
        vcat = jnp.concatenate(
            [vb16[c * CHUNK:(c + 1) * CHUNK, hh * HEAD_D:(hh + 1) * HEAD_D] for c in range(nc)], axis=1)
        heads.append(_dot(w, vcat))
    mixed = jnp.concatenate(
        [jnp.concatenate([heads[hh][:, c * HEAD_D:(c + 1) * HEAD_D] for hh in range(HEADS)], axis=1)
         for c in range(nc)], axis=0)
    ya = u * (mixed + _expand(sb_ref[...], nc))

    @pl.when(j == 0)
    def _():
        pbuf[0:HALO, :] = jnp.zeros((HALO, POOL_W), F32)

    pbuf[HALO:HALO + T, :] = p
    pos = j * T + lax.broadcasted_iota(I32, (T, 1), 0)
    ds = []
    for g, win in enumerate(WINDOWS):
        lo = g * LANE
        ws = p[:, lo:lo + LANE]
        for kk in range(1, win):
            ws = ws + pbuf[HALO - kk:HALO - kk + T, lo:lo + LANE]
        cnt = jnp.minimum(win, pos + 1).astype(F32)
        ds.append(ws / cnt - p[:, lo:lo + LANE])
    d = jnp.concatenate(ds, axis=-1)
    pbuf[0:HALO, :] = p[T - HALO:, :]

    x1, h2 = _mixer_back(x, ya, d, mod_ref, g2_ref, pw_ref, ps_ref, ga_ref, gb_ref, wout_ref, 1)
    x1_ref[...] = x1

    @pl.when(j == n_tiles - 1)
    def _():
        pstate_ref[...] = p[T - HALO:, :]
        cv_ref[...] = v[T - CHUNK:, :]

    if emit_route:
        local, route, seg_n, seg_0 = _route_sort(h2, rw_ref, u_ref)
        l_ref[...] = local
        rt_ref[...] = route
        segn_ref[...] = jnp.broadcast_to(seg_n, (N_EXP, LANE))
        seg0_ref[...] = jnp.broadcast_to(seg_0, (N_EXP, LANE))


def _mixer_sample_body(*refs, n_seq, n_pos, emit_route):
    (x_ref, mod_ref, buf_ref, g1_ref, g2_ref, win_ref, vg_ref, vb_ref, sw_ref, sb_ref,
     pw_ref, ps_ref, ga_ref, gb_ref, wout_ref) = refs[:15]
    k = 15
    if emit_route:
        rw_ref, u_ref = refs[k:k + 2]
        k += 2
    x1_ref = refs[k]
    k += 1
    if emit_route:
        l_ref, rt_ref, segn_ref, seg0_ref = refs[k:k + 4]
        k += 4
    pstate_ref, cv_ref = refs[k:k + 2]

    reps = n_pos
    x = x_ref[...]
    u, v, p = _mixer_front(x, mod_ref, g1_ref, win_ref, vg_ref, vb_ref, reps)

    def slab(a, t):
        return a[t * n_seq:(t + 1) * n_seq, :]

    mixed = []
    for t in range(n_pos):
        acc = sb_ref[t:t + 1, :] + sw_ref[t * n_pos:t * n_pos + 1, :] * slab(v, 0)
        for s in range(1, t + 1):
            acc = acc + sw_ref[t * n_pos + s:t * n_pos + s + 1, :] * slab(v, s)
        mixed.append(acc)
    ya = u * jnp.concatenate(mixed, axis=0)

    hist = [buf_ref[jj] for jj in range(POOL_BUF)] + [slab(p, t) for t in range(n_pos)]
    drows = []
    for t in range(n_pos):
        parts = []
        for g, win in enumerate(WINDOWS):
            lo = g * LANE
            ws = hist[POOL_BUF + t][:, lo:lo + LANE]
            for kk in range(1, win):
                ws = ws + hist[POOL_BUF + t - kk][:, lo:lo + LANE]
            cnt = float(min(win, PAST_LEN + t + 1))
            parts.append(ws / cnt - hist[POOL_BUF + t][:, lo:lo + LANE])
        drows.append(jnp.concatenate(parts, axis=-1))
    d = jnp.concatenate(drows, axis=0)
    for jj in range(POOL_BUF):
        pstate_ref[jj] = hist[n_pos + jj]
    for t in range(n_pos):
        cv_ref[t] = slab(v, t)

    x1, h2 = _mixer_back(x, ya, d, mod_ref, g2_ref, pw_ref, ps_ref, ga_ref, gb_ref, wout_ref, reps)
    x1_ref[...] = x1

    if emit_route:
        n_rt = (n_seq * n_pos) // TR
        for i in range(n_rt):
            local, route, seg_n, seg_0 = _route_sort(h2[i * TR:(i + 1) * TR, :], rw_ref, u_ref)
            l_ref[i * LR:(i + 1) * LR, :] = local
            rt_ref[:, i * TR:(i + 1) * TR] = route
            segn_ref[i] = jnp.broadcast_to(seg_n, (N_EXP, LANE))
            seg0_ref[i] = jnp.broadcast_to(seg_0, (N_EXP, LANE))
        l_ref[n_rt * LR:n_rt * LR + UNIT, :] = jnp.zeros((UNIT, D), F32)


def _const_spec(shape):
    nd = len(shape)
    return pl.BlockSpec(shape, lambda *_: (0,) * nd)


def _rank_matrix():
    return jnp.triu(jnp.ones((TR, TR), BF16), 1)


def _mixer_prompt(x, mod, lw, route_w=None):
    T = TR
    n_b = mod.shape[0]
    n_tok = x.shape[0]
    seq = n_tok // n_b
    n_tiles = seq // T
    n_rt = n_tok // T
    emit_route = route_w is not None
    tile = lambda b, j: b * n_tiles + j
    in_specs = [
        pl.BlockSpec((T, D), lambda b, j: (tile(b, j), 0)),
        pl.BlockSpec((None, 1, 6 * D), lambda b, j: (b, 0, 0)),
    ] + [_const_spec(w.shape) for w in lw]
    args = [x, mod.reshape(n_b, 1, 6 * D)] + list(lw)
    out_shape = [jax.ShapeDtypeStruct((n_tok, D), F32)]
    out_specs = [pl.BlockSpec((T, D), lambda b, j: (tile(b, j), 0))]
    if emit_route:
        umat = _rank_matrix()
        in_specs += [_const_spec(route_w.shape), _const_spec(umat.shape)]
        args += [route_w, umat]
        out_shape += [jax.ShapeDtypeStruct((n_rt * LR, D), F32),
                      jax.ShapeDtypeStruct((N_EXP, n_tok), F32),
                      jax.ShapeDtypeStruct((n_rt, N_EXP, LANE), F32),
                      jax.ShapeDtypeStruct((n_rt, N_EXP, LANE), F32)]
        out_specs += [pl.BlockSpec((LR, D), lambda b, j: (tile(b, j), 0)),
                      pl.BlockSpec((N_EXP, T), lambda b, j: (0, tile(b, j))),
                      pl.BlockSpec((None, N_EXP, LANE), lambda b, j: (tile(b, j), 0, 0)),
                      pl.BlockSpec((None, N_EXP, LANE), lambda b, j: (tile(b, j), 0, 0))]
    out_shape += [jax.ShapeDtypeStruct((n_b, HALO, POOL_W), F32),
                  jax.ShapeDtypeStruct((n_b, CHUNK, SGU_W), F32)]
    out_specs += [pl.BlockSpec((None, HALO, POOL_W), lambda b, j: (b, 0, 0)),
                  pl.BlockSpec((None, CHUNK, SGU_W), lambda b, j: (b, 0, 0))]
    return pl.pallas_call(
        functools.partial(_mixer_prompt_body, T=T, n_tiles=n_tiles, emit_route=emit_route),
        out_shape=out_shape,
        grid=(n_b, n_tiles),
        in_specs=in_specs,
        out_specs=out_specs,
        scratch_shapes=[pltpu.VMEM((HALO + T, POOL_W), F32)],
        compiler_params=pltpu.CompilerParams(
            dimension_semantics=("arbitrary", "arbitrary"), vmem_limit_bytes=VMEM_LIMIT),
        name="mixer_prompt_route" if emit_route else "mixer_prompt",
    )(*args)


def _mixer_sample(x, mod, buf, lw, route_w=None):
    n_seq = mod.shape[0]
    n_tok = x.shape[0]
    n_pos = n_tok // n_seq
    n_rt = n_tok // TR
    emit_route = route_w is not None
    in_specs = [_const_spec(x.shape), _const_spec(mod.shape), _const_spec(buf.shape)]
    in_specs += [_const_spec(w.shape) for w in lw]
    args = [x, mod, buf] + list(lw)
    out_shape = [jax.ShapeDtypeStruct((n_tok, D), F32)]
    if emit_route:
        umat = _rank_matrix()
        in_specs += [_const_spec(route_w.shape), _const_spec(umat.shape)]
        args += [route_w, umat]
        out_shape += [jax.ShapeDtypeStruct((n_rt * LR + UNIT, D), F32),
                      jax.ShapeDtypeStruct((N_EXP, n_tok), F32),
                      jax.ShapeDtypeStruct((n_rt, N_EXP, LANE), F32),
                      jax.ShapeDtypeStruct((n_rt, N_EXP, LANE), F32)]
    out_shape += [jax.ShapeDtypeStruct((POOL_BUF, n_seq, POOL_W), F32),
                  jax.ShapeDtypeStruct((n_pos, n_seq, SGU_W), F32)]
    return pl.pallas_call(
        functools.partial(_mixer_sample_body, n_seq=n_seq, n_pos=n_pos, emit_route=emit_route),
        out_shape=out_shape,
        grid=(1,),
        in_specs=in_specs,
        out_specs=[_const_spec(s.shape) for s in out_shape],
        compiler_params=pltpu.CompilerParams(
            dimension_semantics=("arbitrary",), vmem_limit_bytes=VMEM_LIMIT),
        name="mixer_sample_route" if emit_route else "mixer_sample",
    )(*args)


def _ffn_body(x_ref, mod_ref, g2_ref, w1_ref, w3_ref, w2_ref, o_ref, *, reps):
    x1 = x_ref[...]
    sh2 = _mod_chunk(mod_ref, 3, reps)
    sc2 = _mod_chunk(mod_ref, 4, reps)
    h = (_rmsn(x1) * g2_ref[...] * (1.0 + sc2) + sh2).astype(BF16)
    acc = None
    for c in range(NF):
        a = _dot(h, w1_ref[:, c * TF:(c + 1) * TF])
        bb = _dot(h, w3_ref[:, c * TF:(c + 1) * TF])
        part = _dot((jax.nn.silu(a) * bb).astype(BF16), w2_ref[c * TF:(c + 1) * TF, :])
        acc = part if acc is None else acc + part
    o_ref[...] = x1 + _mod_chunk(mod_ref, 5, reps) * acc


def _mod_spec(mod, tile_rows, tok_per_mod):
    if tok_per_mod:
        spec = pl.BlockSpec((None, 1, 6 * D), lambda i, *_: ((i * tile_rows) // tok_per_mod, 0, 0))
        return mod.reshape(mod.shape[0], 1, 6 * D), spec, 1
    return mod, pl.BlockSpec(mod.shape, lambda i, *_: (0, 0)), tile_rows // mod.shape[0]


def _ffn(x1, mod, g2, w1, w3, w2, *, tm, tok_per_mod):
    n_tok = x1.shape[0]
    mod_in, mod_spec, reps = _mod_spec(mod, tm, tok_per_mod)
    resident = lambda s: pl.BlockSpec(s, lambda i: (0,) * len(s), pipeline_mode=pl.Buffered(1))
    return pl.pallas_call(
        functools.partial(_ffn_body, reps=reps),
        out_shape=jax.ShapeDtypeStruct((n_tok, D), F32),
        grid=(n_tok // tm,),
        in_specs=[pl.BlockSpec((tm, D), lambda i: (i, 0)), mod_spec, _const_spec(g2.shape),
                  resident(w1.shape), resident(w3.shape), resident(w2.shape)],
        out_specs=pl.BlockSpec((tm, D), lambda i: (i, 0)),
        compiler_params=pltpu.CompilerParams(
            dimension_semantics=("arbitrary",), vmem_limit_bytes=VMEM_LIMIT),
        name="ffn_dense",
    )(x1, mod_in, g2, w1, w3, w2)


def _moe_body(se_ref, row0_ref, nsub_ref, nreal_ref, usrc_ref, tl0_ref, tln_ref,
              lp_ref, ls_ref, w1_ref, w3_ref, w2_ref, yp_ref, ys_ref,
              xbuf, acc, sem_in, sem_out, *, units_p, zero_unit, n_rt):
    g = pl.program_id(0)
    f = pl.program_id(1)
    n = nsub_ref[g]
    u0 = row0_ref[g] // UNIT
    per_sub = SUBM // UNIT
    batch = 4

    def rows(unit):
        return pl.ds(pl.multiple_of(unit * UNIT, UNIT), UNIT)

    def on_owner(unit, fn):
        @pl.when(unit < units_p)
        def _():
            fn(0, unit)

        @pl.when(unit >= units_p)
        def _():
            fn(1, unit - units_p)

    def y_copy(owner, unit, src):
        return pltpu.make_async_copy(src, (yp_ref, ys_ref)[owner].at[rows(unit)], sem_out)

    def x_copy(owner, unit, u):
        return pltpu.make_async_copy((lp_ref, ls_ref)[owner].at[rows(unit)], xbuf.at[rows(u)], sem_in)

    def wait_rows(sem, count, m):
        def one(i, c):
            pltpu.make_async_copy(lp_ref.at[pl.ds(0, m)], xbuf.at[pl.ds(0, m)], sem).wait()
            return c
        lax.fori_loop(0, count, one, 0)

    @pl.when((g == 0) & (f == 0))
    def _():
        acc[0:UNIT, :] = jnp.zeros((UNIT, D), F32)
        for phase in ("start", "wait"):
            def tile_tail(i, c):
                def one(k, c2):
                    on_owner(tl0_ref[i] + k,
                             lambda w, unit: getattr(y_copy(w, unit, acc.at[pl.ds(0, UNIT)]), phase)())
                    return c2
                lax.fori_loop(0, tln_ref[i], one, 0)
                return c
            lax.fori_loop(0, n_rt, tile_tail, 0)

    @pl.when((f == 0) & (n > 0))
    def _():
        def start(i, c):
            for k in range(batch):
                u = i * batch + k
                on_owner(usrc_ref[u0 + u], lambda w, unit: x_copy(w, unit, u).start())
            return c
        lax.fori_loop(0, n * (per_sub // batch), start, 0)
        wait_rows(sem_in, n, SUBM)

    def run(first):
        w1b = w1_ref[...].astype(BF16)
        w3b = w3_ref[...].astype(BF16)
        w2b = w2_ref[...].astype(BF16)

        def block(r, m):
            h = xbuf[pl.ds(r, m), :].astype(BF16)
            a = _dot(h, w1b)
            bb = _dot(h, w3b)
            part = _dot((jax.nn.silu(a) * bb).astype(BF16), w2b)
            if first:
                acc[pl.ds(r, m), :] = part
            else:
                acc[pl.ds(r, m), :] = acc[pl.ds(r, m), :] + part

        block(0, SUBM)

        def pair(i, c):
            block(pl.multiple_of(SUBM + i * 2 * SUBM, SUBM), 2 * SUBM)
            return c
        lax.fori_loop(0, lax.shift_right_logical(n - 1, 1), pair, 0)

        @pl.when((n & 1) == 0)
        def _():
            block(pl.multiple_of((n - 1) * SUBM, SUBM), SUBM)

    @pl.when((f == 0) & (n > 0))
    def _():
        run(True)

    @pl.when((f > 0) & (n > 0))
    def _():
        run(False)

    @pl.when((f == NF - 1) & (n > 0))
    def _():
        def start(i, c):
            for k in range(batch):
                u = i * batch + k
                unit = usrc_ref[u0 + u]

                @pl.when(unit != zero_unit)
                def _():
                    on_owner(unit, lambda w, dst: y_copy(w, dst, acc.at[rows(u)]).start())
            return c
        lax.fori_loop(0, n * (per_sub // batch), start, 0)
        n_real = nreal_ref[g]
        n_big = lax.shift_right_logical(n_real, per_sub.bit_length() - 1)
        wait_rows(sem_out, n_big, SUBM)
        wait_rows(sem_out, n_real - n_big * per_sub, UNIT)


def _moe_grouped(l_p, l_s, w1, w3, w2, plan):
    st_e, st_row0, st_n, st_real, usrc, tl0, tln = plan
    n_st = st_e.shape[0]
    n_rt = tl0.shape[0]
    units_p = l_p.shape[0] // UNIT
    n_y_s = l_s.shape[0] - UNIT

    def fidx(g, f, ns):
        return jnp.where(ns[g] > 0, f, NF - 1)

    grid_spec = pltpu.PrefetchScalarGridSpec(
        num_scalar_prefetch=len(plan),
        grid=(n_st, NF),
        in_specs=[
            pl.BlockSpec(memory_space=pl.ANY),
            pl.BlockSpec(memory_space=pl.ANY),
            pl.BlockSpec((None, D, TF), lambda g, f, se, r0, ns, *_: (se[g], 0, fidx(g, f, ns))),
            pl.BlockSpec((None, D, TF), lambda g, f, se, r0, ns, *_: (se[g], 0, fidx(g, f, ns))),
            pl.BlockSpec((None, TF, D), lambda g, f, se, r0, ns, *_: (se[g], fidx(g, f, ns), 0)),
        ],
        out_specs=[pl.BlockSpec(memory_space=pl.ANY), pl.BlockSpec(memory_space=pl.ANY)],
        scratch_shapes=[pltpu.VMEM((SPT * SUBM, D), F32), pltpu.VMEM((SPT * SUBM, D), F32),
                        pltpu.SemaphoreType.DMA(()), pltpu.SemaphoreType.DMA(())],
    )
    return pl.pallas_call(
        functools.partial(_moe_body, units_p=units_p, zero_unit=(l_p.shape[0] + n_y_s) // UNIT, n_rt=n_rt),
        out_shape=[jax.ShapeDtypeStruct(l_p.shape, F32), jax.ShapeDtypeStruct((n_y_s, D), F32)],
        grid_spec=grid_spec,
        compiler_params=pltpu.CompilerParams(
            dimension_semantics=("arbitrary", "arbitrary"), vmem_limit_bytes=VMEM_LIMIT,
            has_side_effects=True),
        name="moe_grouped",
    )(*plan, l_p, l_s, w1, w3, w2)


def _combine_body(rt_ref, x_ref, mod_ref, fg_ref, y_ref, o_ref, *, reps, final):
    tc = x_ref.shape[0]

    def column(k):
        return jnp.transpose(jnp.broadcast_to(rt_ref[k:k + 1, :], (LANE, tc)))

    lane_id = lax.broadcasted_iota(I32, (tc, LR), 1).astype(F32)
    yb = y_ref[...].astype(BF16)

    def pick(k):
        slot = jnp.concatenate([column(k)] * (LR // LANE), axis=1)
        onehot = jnp.where(lane_id == slot, 1.0, 0.0).astype(BF16)
        return _dot(onehot, yb)

    g1 = jnp.concatenate([column(2)] * (D // LANE), axis=1)
    g2 = jnp.concatenate([column(3)] * (D // LANE), axis=1)
    x2 = x_ref[...] + _mod_chunk(mod_ref, 5, reps) * (g1 * pick(0) + g2 * pick(1))
    o_ref[...] = _rmsn(x2) * fg_ref[...] if final else x2


def _combine(x1, mod, final_g, y_local, route, *, tok_per_mod, final):
    n_tok = x1.shape[0]
    tc = TR
    mod_in, mod_spec, reps = _mod_spec(mod, tc, tok_per_mod)
    return pl.pallas_call(
        functools.partial(_combine_body, reps=reps, final=final),
        out_shape=jax.ShapeDtypeStruct((n_tok, D), F32),
        grid=(n_tok // tc,),
        in_specs=[pl.BlockSpec((N_EXP, tc), lambda i: (0, i)),
                  pl.BlockSpec((tc, D), lambda i: (i, 0)),
                  mod_spec,
                  _const_spec((1, D)),
                  pl.BlockSpec((LR, D), lambda i: (i, 0))],
        out_specs=pl.BlockSpec((tc, D), lambda i: (i, 0)),
        compiler_params=pltpu.CompilerParams(
            dimension_semantics=("arbitrary",), vmem_limit_bytes=VMEM_LIMIT),
        name="moe_combine",
    )(route, x1, mod_in, final_g, y_local)


def _final_norm_body(x_ref, fg_ref, o_ref):
    o_ref[...] = _rmsn(x_ref[...]) * fg_ref[...]


def _final_norm(x, final_g, *, tm=512):
    n_tok = x.shape[0]
    return pl.pallas_call(
        _final_norm_body,
        out_shape=jax.ShapeDtypeStruct((n_tok, D), F32),
        grid=(n_tok // tm,),
        in_specs=[pl.BlockSpec((tm, D), lambda i: (i, 0)), _const_spec((1, D))],
        out_specs=pl.BlockSpec((tm, D), lambda i: (i, 0)),
        name="final_norm",
    )(x, final_g)


def _expert_plan(seg_n, seg_0):
    n_rt = seg_n.shape[0]
    seg_n = seg_n.astype(I32)
    seg_0 = seg_0.astype(I32)
    total = jnp.sum(seg_n, axis=0)
    nsub = (total + SUBM - 1) // SUBM
    off = (jnp.cumsum(nsub) - nsub) * SUBM
    dst = off[None, :] + jnp.cumsum(seg_n, axis=0) - seg_n
    src = jnp.arange(n_rt, dtype=I32)[:, None] * LR + seg_0
    s_start = dst.T.reshape(-1) // UNIT
    s_len = seg_n.T.reshape(-1) // UNIT
    s_src = src.T.reshape(-1) // UNIT
    max_sub = (n_rt * (TOP_K * TR + N_EXP * (UNIT - 1))) // SUBM + N_EXP
    zero_unit = n_rt * LR // UNIT
    take = lambda table, idx: jnp.sum(
        jnp.where(idx[:, None] == jnp.arange(table.shape[0], dtype=I32)[None, :], table[None, :], 0), axis=1)
    unit = jnp.arange(max_sub * SUBM // UNIT, dtype=I32)
    owner = jnp.sum((s_start[None, :] <= unit[:, None]).astype(I32), axis=1) - 1
    owner = jnp.clip(owner, 0, s_start.shape[0] - 1)
    k = unit - take(s_start, owner)
    real = (k >= 0) & (k < take(s_len, owner))
    usrc = jnp.where(real, take(s_src, owner) + k, zero_unit)
    nst = (nsub + SPT - 1) // SPT
    n_st = (max_sub + N_EXP * (SPT - 1)) // SPT
    st_end = jnp.cumsum(nst)
    g = jnp.arange(n_st, dtype=I32)
    n_used = st_end[-1]
    valid = g < n_used
    which = lambda q: jnp.minimum(jnp.sum((st_end[None, :] <= q[:, None]).astype(I32), axis=1), N_EXP - 1)
    e = jnp.where(valid, which(g), which((n_used - 1).reshape(1)))
    nst_e, nsub_e = take(nst, e), take(nsub, e)
    kk = g - (take(st_end, e) - nst_e)
    base = nsub_e // jnp.maximum(nst_e, 1)
    rem = nsub_e - base * nst_e
    size = jnp.where(valid, base + (kk < rem).astype(I32), 0)
    row0 = jnp.where(valid, take(off, e) + (kk * base + jnp.minimum(kk, rem)) * SUBM, 0)
    first = row0 // UNIT
    inside = (unit[None, :] >= first[:, None]) & (unit[None, :] < (first + size * (SUBM // UNIT))[:, None])
    n_real = jnp.sum((inside & real[None, :]).astype(I32), axis=1)
    used = jnp.sum(seg_n, axis=1)
    tail0 = (jnp.arange(n_rt, dtype=I32) * LR + used) // UNIT
    tailn = (LR - used) // UNIT
    as_i32 = lambda *xs: tuple(x.astype(I32) for x in xs)
    return as_i32(e, row0, size, n_real, usrc, tail0, tailn)


def _layer_weights(l, sample, norm1_g, norm2_g, w_in, v_norm_g, v_norm_b, sgu_w, sgu_b,
                   pool_w, pool_scale, branch_a_g, branch_b_g, w_out, n_pos):
    row = lambda a: a[l].reshape(1, -1)
    if sample:
        w8 = jnp.transpose(sgu_w[l][:, :n_pos, :n_pos], (1, 2, 0))
        sw = jnp.repeat(w8, HEAD_D, axis=2).reshape(n_pos * n_pos, SGU_W)
        sb = jnp.repeat(sgu_b[l][:, :n_pos].T, HEAD_D, axis=1)
    else:
        sw = sgu_w[l]
        sb = jnp.repeat(sgu_b[l].T, HEAD_D, axis=1)
    pw = pool_w[l].astype(BF16)
    z = jnp.zeros((LANE, LANE), BF16)
    pw2 = jnp.stack([jnp.block([[pw[0], z], [z, pw[1]]]), jnp.block([[pw[2], z], [z, pw[3]]])])
    return [row(norm1_g), row(norm2_g), w_in[l].astype(BF16), row(v_norm_g), row(v_norm_b), sw, sb,
            pw2, row(pool_scale), row(branch_a_g), row(branch_b_g), w_out[l].astype(BF16)]


def kernel(x_prompt, x_sample, state_pool, c_prompt, c_sample, norm1_g, norm2_g, ada_w, ada_b, w_in, v_norm_g, v_norm_b, sgu_w, sgu_b, pool_w, pool_scale, branch_a_g, branch_b_g, w_out, ffn_w1, ffn_w3, ffn_w2, router_w, moe_w1, moe_w3, moe_w2, final_g):
    n_b, seq, _ = x_prompt.shape
    n_seq, n_pos, _ = x_sample.shape
    depth = ada_w.shape[0]
    n_p = n_b * seq
    n_s = n_seq * n_pos

    mod = _adaln(jnp.concatenate([c_prompt, c_sample], axis=0), ada_w, ada_b)
    mod_p, mod_s = mod[:, :n_b], mod[:, n_b:]

    xp = x_prompt.reshape(n_p, D)
    xs = jnp.swapaxes(x_sample, 0, 1).reshape(n_s, D)
    bufs = jnp.transpose(state_pool, (0, 2, 1, 3))
    fg = final_g.reshape(1, D)
    per_layer = (norm1_g, norm2_g, w_in, v_norm_g, v_norm_b, sgu_w, sgu_b, pool_w, pool_scale,
                 branch_a_g, branch_b_g, w_out)

    pool_p, pool_s, cv_p, cv_s = [], [], [], []
    normed = False
    for l in range(depth):
        lw_p = _layer_weights(l, False, *per_layer, n_pos)
        lw_s = _layer_weights(l, True, *per_layer, n_pos)
        g2 = norm2_g[l].reshape(1, D)
        if l % 2 == 0:
            x1p, pp, vp = _mixer_prompt(xp, mod_p[l], lw_p)
            x1s, ps, vs = _mixer_sample(xs, mod_s[l], bufs[l], lw_s)
            w1, w3, w2 = (w[l // 2].astype(BF16) for w in (ffn_w1, ffn_w3, ffn_w2))
            xp = _ffn(x1p, mod_p[l], g2, w1, w3, w2, tm=512, tok_per_mod=seq)
            xs = _ffn(x1s, mod_s[l], g2, w1, w3, w2, tm=2 * n_seq, tok_per_mod=0)
            normed = False
        else:
            m = l // 2
            rw = jnp.zeros((2 * N_EXP, D), BF16).at[:N_EXP].set(router_w[m].T.astype(BF16))
            x1p, l_p, rt_p, segn_p, seg0_p, pp, vp = _mixer_prompt(xp, mod_p[l], lw_p, route_w=rw)
            x1s, l_s, rt_s, segn_s, seg0_s, ps, vs = _mixer_sample(xs, mod_s[l], bufs[l], lw_s, route_w=rw)
            plan = _expert_plan(jnp.concatenate([segn_p[:, :, 0], segn_s[:, :, 0]], axis=0),
                                jnp.concatenate([seg0_p[:, :, 0], seg0_s[:, :, 0]], axis=0))
            y_p, y_s = _moe_grouped(l_p, l_s, moe_w1[m], moe_w3[m], moe_w2[m], plan)
            normed = l == depth - 1
            xp = _combine(x1p, mod_p[l], fg, y_p, rt_p, tok_per_mod=seq, final=normed)
            xs = _combine(x1s, mod_s[l], fg, y_s, rt_s, tok_per_mod=0, final=normed)
        pool_p.append(pp[:, HALO - POOL_BUF:])
        pool_s.append(jnp.swapaxes(ps, 0, 1))
        cv_p.append(vp)
        cv_s.append(jnp.swapaxes(vs, 0, 1))

    if not normed:
        xp = _final_norm(xp, fg)
        xs = _final_norm(xs, fg)
    y_prompt = xp.reshape(n_b, seq, D)
    y_sample = jnp.swapaxes(xs.reshape(n_pos, n_seq, D), 0, 1)
    return (y_prompt, y_sample, jnp.stack(pool_p), jnp.stack(pool_s), jnp.stack(cv_p), jnp.stack(cv_s))
```

```python
import functools

import jax
import jax.numpy as jnp
from jax import lax
from jax.experimental import pallas as pl
from jax.experimental.pallas import tpu as pltpu

F32 = jnp.float32
BF16 = jnp.bfloat16
I32 = jnp.int32

D = 1024
SGU_W = 512
POOL_W = 512
HEADS = 4
HEAD_D = 128
CHUNK = 128
WINDOWS = (2, 4, 8, 16)
POOL_BUF = 15
HALO = 16
IN_W = 2 * SGU_W + POOL_W
D_FF = 3584
N_EXP = 8
TOP_K = 2
EPS = 1e-6
PAST_LEN = 16384

LANE = 128
SUBLANE = 8
TF = 512
NF = D_FF // TF
TR = 512
UNIT = SUBLANE
LR = -(-(TOP_K * TR + N_EXP * (UNIT - 1)) // LANE) * LANE
SUBM = 256
SPT = 8
VMEM_LIMIT = 56 * 1024 * 1024


def _rmsn(x):
    return x * lax.rsqrt(jnp.mean(x * x, axis=-1, keepdims=True) + EPS)


def _expand(m, reps):
    return m if reps == 1 else jnp.concatenate([m] * reps, axis=0)


def _mod_chunk(mod_ref, i, reps):
    return _expand(mod_ref[:, i * D:(i + 1) * D], reps)


def _dot(a, b):
    return jnp.dot(a, b, preferred_element_type=F32)


def _adaln_body(c_ref, w_ref, b_ref, o_ref):
    s = jax.nn.silu(c_ref[...]).astype(BF16)
    o_ref[...] = _dot(s, w_ref[...].astype(BF16)) + b_ref[...]


def _adaln(c_all, ada_w, ada_b):
    depth, _, n_out = ada_w.shape
    rows = c_all.shape[0]
    tn = 1024
    return pl.pallas_call(
        _adaln_body,
        out_shape=jax.ShapeDtypeStruct((depth, rows, n_out), F32),
        grid=(depth, n_out // tn),
        in_specs=[
            pl.BlockSpec((rows, D), lambda l, n: (0, 0)),
            pl.BlockSpec((None, D, tn), lambda l, n: (l, 0, n)),
            pl.BlockSpec((None, 1, tn), lambda l, n: (l, 0, n)),
        ],
        out_specs=pl.BlockSpec((None, rows, tn), lambda l, n: (l, 0, n)),
        compiler_params=pltpu.CompilerParams(
            dimension_semantics=("arbitrary", "arbitrary"), vmem_limit_bytes=VMEM_LIMIT),
        name="adaln",
    )(c_all, ada_w, ada_b.reshape(depth, 1, n_out))


def _mixer_front(x, mod_ref, g1_ref, win_ref, vg_ref, vb_ref, reps):
    sh1 = _mod_chunk(mod_ref, 0, reps)
    sc1 = _mod_chunk(mod_ref, 1, reps)
    h = _rmsn(x) * g1_ref[...] * (1.0 + sc1) + sh1
    z = _dot(h.astype(BF16), win_ref[...])
    u = jax.nn.gelu(z[:, :SGU_W])
    vr = jax.nn.gelu(z[:, SGU_W:2 * SGU_W])
    p = z[:, 2 * SGU_W:]
    vs = []
    for hh in range(HEADS):
        vh = vr[:, hh * HEAD_D:(hh + 1) * HEAD_D]
        dlt = vh - jnp.mean(vh, axis=-1, keepdims=True)
        var = jnp.mean(dlt * dlt, axis=-1, keepdims=True)
        vs.append(dlt * lax.rsqrt(var + EPS))
    v = jnp.concatenate(vs, axis=-1) * vg_ref[...] + vb_ref[...]
    return u, v, p


def _mixer_back(x, ya, d, mod_ref, g2_ref, pw_ref, ps_ref, ga_ref, gb_ref, wout_ref, reps):
    db = d.astype(BF16)
    yb = jnp.concatenate(
        [_dot(db[:, :2 * LANE], pw_ref[0]), _dot(db[:, 2 * LANE:], pw_ref[1])], axis=-1) * ps_ref[...]
    mixin = jnp.concatenate([_rmsn(ya) * ga_ref[...], _rmsn(yb) * gb_ref[...]], axis=-1)
    mix = _dot(mixin.astype(BF16), wout_ref[...])
    x1 = x + _mod_chunk(mod_ref, 2, reps) * mix
    sh2 = _mod_chunk(mod_ref, 3, reps)
    sc2 = _mod_chunk(mod_ref, 4, reps)
    h2 = _rmsn(x1) * g2_ref[...] * (1.0 + sc2) + sh2
    return x1, h2


def _route_sort(h2, rw_ref, u_ref):
    t = h2.shape[0]
    hb = h2.astype(BF16)
    logits = lax.dot_general(rw_ref[...], hb, (((1,), (1,)), ((), ())),
                             preferred_element_type=F32)[:N_EXP]
    sub = lax.broadcasted_iota(I32, (N_EXP, t), 0).astype(F32)
    m1 = jnp.max(logits, axis=0, keepdims=True)
    i1 = jnp.min(jnp.where(logits == m1, sub, float(N_EXP)), axis=0, keepdims=True)
    rest = jnp.where(sub == i1, -jnp.inf, logits)
    m2 = jnp.max(rest, axis=0, keepdims=True)
    i2 = jnp.min(jnp.where(rest == m2, sub, float(N_EXP)), axis=0, keepdims=True)
    e2 = jnp.exp(m2 - m1)
    den = 1.0 + e2
    g1 = 1.0 / den
    g2 = e2 / den
    sub16 = lax.broadcasted_iota(I32, (2 * N_EXP, t), 0).astype(F32)
    mask16 = jnp.where((sub16 == i1) | (sub16 == i2), 1.0, 0.0)
    rank = _dot(mask16.astype(BF16), u_ref[...])[:N_EXP]
    cnt = jnp.sum(mask16[:N_EXP], axis=1, keepdims=True)
    cnt_pad = jnp.floor((cnt + float(UNIT - 1)) * (1.0 / UNIT)) * float(UNIT)
    sub1 = sub[:, 0:1]
    lo = jnp.zeros((N_EXP, 1), F32)
    for e in range(N_EXP - 1):
        lo = lo + jnp.where(sub1 > float(e), cnt_pad[e:e + 1, :], 0.0)
    base = lo + rank
    ls1 = jnp.sum(jnp.where(sub == i1, base, 0.0), axis=0, keepdims=True)
    ls2 = jnp.sum(jnp.where(sub == i2, base, 0.0), axis=0, keepdims=True)
    srow = lax.broadcasted_iota(I32, (LR, t), 0).astype(F32)
    perm = jnp.where((srow == ls1) | (srow == ls2), 1.0, 0.0).astype(BF16)
    local = _dot(perm, hb)
    zf = jnp.zeros((N_EXP, t), F32)
    route = jnp.where(sub == 0.0, ls1, jnp.where(sub == 1.0, ls2, jnp.where(
        sub == 2.0, g1, jnp.where(sub == 3.0, g2, zf))))
    return local, route, cnt_pad, lo


def _mixer_prompt_body(*refs, T, n_tiles, emit_route):
    (x_ref, mod_ref, g1_ref, g2_ref, win_ref, vg_ref, vb_ref, sw_ref, sb_ref,
     pw_ref, ps_ref, ga_ref, gb_ref, wout_ref) = refs[:14]
    k = 14
    if emit_route:
        rw_ref, u_ref = refs[k:k + 2]
        k += 2
    x1_ref = refs[k]
    k += 1
    if emit_route:
        l_ref, rt_ref, segn_ref, seg0_ref = refs[k:k + 4]
        k += 4
    pstate_ref, cv_ref, pbuf = refs[k:k + 3]

    j = pl.program_id(1)
    x = x_ref[...]
    u, v, p = _mixer_front(x, mod_ref, g1_ref, win_ref, vg_ref, vb_ref, 1)

    nc = T // CHUNK
    row = lax.broadcasted_iota(I32, (CHUNK, CHUNK), 0)
    col = lax.broadcasted_iota(I32, (CHUNK, CHUNK), 1)
    vb16 = v.astype(BF16)
    heads = []
    for hh in range(HEADS):
        w = jnp.where(row >= col, sw_ref[hh], 0.0).astype(BF16)
        vcat = jnp.concatenate(
            [vb16[c * CHUNK:(c + 1) * CHUNK, hh * HEAD_D:(hh + 1) * HEAD_D] for c in range(nc)], axis=1)
        heads.append(_dot(w, vcat))
    mixed = jnp.concatenate(
        [jnp.concatenate([heads[hh][:, c * HEAD_D:(c + 1) * HEAD_D] for hh in range(HEADS)], axis=1)
         for c in range(nc)], axis=0)
    ya = u * (mixed + _expand(sb_ref[...], nc))

    @pl.when(j == 0)
    def _():
        pbuf[0:HALO, :] = jnp.zeros((HALO, POOL_W), F32)

    pbuf[HALO:HALO + T, :] = p
    pos = j * T + lax.broadcasted_iota(I32, (T, 1), 0)
    ds = []
    for g, win in enumerate(WINDOWS):
        lo = g * LANE
        ws = p[:, lo:lo + LANE]
        for kk in range(1, win):
            ws = ws + pbuf[HALO - kk:HALO - kk + T, lo:lo + LANE]
        cnt = jnp.minimum(win, pos + 1).astype(F32)
        ds.append(ws / cnt - p[:, lo:lo + LANE])
    d = jnp.concatenate(ds, axis=-1)
    pbuf[0:HALO, :] = p[T - HALO:, :]

    x1, h2 = _mixer_back(x, ya, d, mod_ref, g2_ref, pw_ref, ps_ref, ga_ref, gb_ref, wout_ref, 1)
    x1_ref[...] = x1

    @pl.when(j == n_tiles - 1)
    def _():
        pstate_ref[...] = p[T - HALO:, :]
        cv_ref[...] = v[T - CHUNK:, :]

    if emit_route:
        local, route, seg_n, seg_0 = _route_sort(h2, rw_ref, u_ref)
        l_ref[...] = local
        rt_ref[...] = route
        segn_ref[...] = jnp.broadcast_to(seg_n, (N_EXP, LANE))
        seg0_ref[...] = jnp.broadcast_to(seg_0, (N_EXP, LANE))


def _mixer_sample_body(*refs, n_seq, n_pos, emit_route):
    (x_ref, mod_ref, buf_ref, g1_ref, g2_ref, win_ref, vg_ref, vb_ref, sw_ref, sb_ref,
     pw_ref, ps_ref, ga_ref, gb_ref, wout_ref) = refs[:15]
    k = 15
    if emit_route:
        rw_ref, u_ref = refs[k:k + 2]
        k += 2
    x1_ref = refs[k]
    k += 1
    if emit_route:
        l_ref, rt_ref, segn_ref, seg0_ref = refs[k:k + 4]
        k += 4
    pstate_ref, cv_ref = refs[k:k + 2]

    reps = n_pos
    x = x_ref[...]
    u, v, p = _mixer_front(x, mod_ref, g1_ref, win_ref, vg_ref, vb_ref, reps)

    def slab(a, t):
        return a[t * n_seq:(t + 1) * n_seq, :]

    mixed = []
    for t in range(n_pos):
        acc = sb_ref[t:t + 1, :] + sw_ref[t * n_pos:t * n_pos + 1, :] * slab(v, 0)
        for s in range(1, t + 1):
            acc = acc + sw_ref[t * n_pos + s:t * n_pos + s + 1, :] * slab(v, s)
        mixed.append(acc)
    ya = u * jnp.concatenate(mixed, axis=0)

    hist = [buf_ref[jj] for jj in range(POOL_BUF)] + [slab(p, t) for t in range(n_pos)]
    drows = []
    for t in range(n_pos):
        parts = []
        for g, win in enumerate(WINDOWS):
            lo = g * LANE
            ws = hist[POOL_BUF + t][:, lo:lo + LANE]
            for kk in range(1, win):
                ws = ws + hist[POOL_BUF + t - kk][:, lo:lo + LANE]
            cnt = float(min(win, PAST_LEN + t + 1))
            parts.append(ws / cnt - hist[POOL_BUF + t][:, lo:lo + LANE])
        drows.append(jnp.concatenate(parts, axis=-1))
    d = jnp.concatenate(drows, axis=0)
    for jj in range(POOL_BUF):
        pstate_ref[jj] = hist[n_pos + jj]
    for t in range(n_pos):
        cv_ref[t] = slab(v, t)

    x1, h2 = _mixer_back(x, ya, d, mod_ref, g2_ref, pw_ref, ps_ref, ga_ref, gb_ref, wout_ref, reps)
    x1_ref[...] = x1

    if emit_route:
        n_rt = (n_seq * n_pos) // TR
        for i in range(n_rt):
            local, route, seg_n, seg_0 = _route_sort(h2[i * TR:(i + 1) * TR, :], rw_ref, u_ref)
            l_ref[i * LR:(i + 1) * LR, :] = local
            rt_ref[:, i * TR:(i + 1) * TR] = route
            segn_ref[i] = jnp.broadcast_to(seg_n, (N_EXP, LANE))
            seg0_ref[i] = jnp.broadcast_to(seg_0, (N_EXP, LANE))
        l_ref[n_rt * LR:n_rt * LR + UNIT, :] = jnp.zeros((UNIT, D), F32)


def _const_spec(shape):
    nd = len(shape)
    return pl.BlockSpec(shape, lambda *_: (0,) * nd)


def _rank_matrix():
    return jnp.triu(jnp.ones((TR, TR), BF16), 1)


def _mixer_prompt(x, mod, lw, route_w=None):
    T = TR
    n_b = mod.shape[0]
    n_tok = x.shape[0]
    seq = n_tok // n_b
    n_tiles = seq // T
    n_rt = n_tok // T
    emit_route = route_w is not None
    tile = lambda b, j: b * n_tiles + j
    in_specs = [
        pl.BlockSpec((T, D), lambda b, j: (tile(b, j), 0)),
        pl.BlockSpec((None, 1, 6 * D), lambda b, j: (b, 0, 0)),
    ] + [_const_spec(w.shape) for w in lw]
    args = [x, mod.reshape(n_b, 1, 6 * D)] + list(lw)
    out_shape = [jax.ShapeDtypeStruct((n_tok, D), F32)]
    out_specs = [pl.BlockSpec((T, D), lambda b, j: (tile(b, j), 0))]
    if emit_route:
        umat = _rank_matrix()
        in_specs += [_const_spec(route_w.shape), _const_spec(umat.shape)]
        args += [route_w, umat]
        out_shape += [jax.ShapeDtypeStruct((n_rt * LR, D), F32),
                      jax.ShapeDtypeStruct((N_EXP, n_tok), F32),
                      jax.ShapeDtypeStruct((n_rt, N_EXP, LANE), F32),
                      jax.ShapeDtypeStruct((n_rt, N_EXP, LANE), F32)]
        out_specs += [pl.BlockSpec((LR, D), lambda b, j: (tile(b, j), 0)),
                      pl.BlockSpec((N_EXP, T), lambda b, j: (0, tile(b, j))),
                      pl.BlockSpec((None, N_EXP, LANE), lambda b, j: (tile(b, j), 0, 0)),
                      pl.BlockSpec((None, N_EXP, LANE), lambda b, j: (tile(b, j), 0, 0))]
    out_shape += [jax.ShapeDtypeStruct((n_b, HALO, POOL_W), F32),
                  jax.ShapeDtypeStruct((n_b, CHUNK, SGU_W), F32)]
    out_specs += [pl.BlockSpec((None, HALO, POOL_W), lambda b, j: (b, 0, 0)),
                  pl.BlockSpec((None, CHUNK, SGU_W), lambda b, j: (b, 0, 0))]
    return pl.pallas_call(
        functools.partial(_mixer_prompt_body, T=T, n_tiles=n_tiles, emit_route=emit_route),
        out_shape=out_shape,
        grid=(n_b, n_tiles),
        in_specs=in_specs,
        out_specs=out_specs,
        scratch_shapes=[pltpu.VMEM((HALO + T, POOL_W), F32)],
        compiler_params=pltpu.CompilerParams(
            dimension_semantics=("arbitrary", "arbitrary"), vmem_limit_bytes=VMEM_LIMIT),
        name="mixer_prompt_route" if emit_route else "mixer_prompt",
    )(*args)


def _mixer_sample(x, mod, buf, lw, route_w=None):
    n_seq = mod.shape[0]
    n_tok = x.shape[0]
    n_pos = n_tok // n_seq
    n_rt = n_tok // TR
    emit_route = route_w is not None
    in_specs = [_const_spec(x.shape), _const_spec(mod.shape), _const_spec(buf.shape)]
    in_specs += [_const_spec(w.shape) for w in lw]
    args = [x, mod, buf] + list(lw)
    out_shape = [jax.ShapeDtypeStruct((n_tok, D), F32)]
    if emit_route:
        umat = _rank_matrix()
        in_specs += [_const_spec(route_w.shape), _const_spec(umat.shape)]
        args += [route_w, umat]
        out_shape += [jax.ShapeDtypeStruct((n_rt * LR + UNIT, D), F32),
                      jax.ShapeDtypeStruct((N_EXP, n_tok), F32),
                      jax.ShapeDtypeStruct((n_rt, N_EXP, LANE), F32),
                      jax.ShapeDtypeStruct((n_rt, N_EXP, LANE), F32)]
    out_shape += [jax.ShapeDtypeStruct((POOL_BUF, n_seq, POOL_W), F32),
                  jax.ShapeDtypeStruct((n_pos, n_seq, SGU_W), F32)]
    return pl.pallas_call(
        functools.partial(_mixer_sample_body, n_seq=n_seq, n_pos=n_pos, emit_route=emit_route),
        out_shape=out_shape,
        grid=(1,),
        in_specs=in_specs,
        out_specs=[_const_spec(s.shape) for s in out_shape],
        compiler_params=pltpu.CompilerParams(
            dimension_semantics=("arbitrary",), vmem_limit_bytes=VMEM_LIMIT),
        name="mixer_sample_route" if emit_route else "mixer_sample",
    )(*args)


def _ffn_body(x_ref, mod_ref, g2_ref, w1_ref, w3_ref, w2_ref, o_ref, *, reps):
    x1 = x_ref[...]
    sh2 = _mod_chunk(mod_ref, 3, reps)
    sc2 = _mod_chunk(mod_ref, 4, reps)
    h = (_rmsn(x1) * g2_ref[...] * (1.0 + sc2) + sh2).astype(BF16)
    acc = None
    for c in range(NF):
        a = _dot(h, w1_ref[:, c * TF:(c + 1) * TF])
        bb = _dot(h, w3_ref[:, c * TF:(c + 1) * TF])
        part = _dot((jax.nn.silu(a) * bb).astype(BF16), w2_ref[c * TF:(c + 1) * TF, :])
        acc = part if acc is None else acc + part
    o_ref[...] = x1 + _mod_chunk(mod_ref, 5, reps) * acc


def _mod_spec(mod, tile_rows, tok_per_mod):
    if tok_per_mod:
        spec = pl.BlockSpec((None, 1, 6 * D), lambda i, *_: ((i * tile_rows) // tok_per_mod, 0, 0))
        return mod.reshape(mod.shape[0], 1, 6 * D), spec, 1
    return mod, pl.BlockSpec(mod.shape, lambda i, *_: (0, 0)), tile_rows // mod.shape[0]


def _ffn(x1, mod, g2, w1, w3, w2, *, tm, tok_per_mod):
    n_tok = x1.shape[0]
    mod_in, mod_spec, reps = _mod_spec(mod, tm, tok_per_mod)
    resident = lambda s: pl.BlockSpec(s, lambda i: (0,) * len(s), pipeline_mode=pl.Buffered(1))
    return pl.pallas_call(
        functools.partial(_ffn_body, reps=reps),
        out_shape=jax.ShapeDtypeStruct((n_tok, D), F32),
        grid=(n_tok // tm,),
        in_specs=[pl.BlockSpec((tm, D), lambda i: (i, 0)), mod_spec, _const_spec(g2.shape),
                  resident(w1.shape), resident(w3.shape), resident(w2.shape)],
        out_specs=pl.BlockSpec((tm, D), lambda i: (i, 0)),
        compiler_params=pltpu.CompilerParams(
            dimension_semantics=("arbitrary",), vmem_limit_bytes=VMEM_LIMIT),
        name="ffn_dense",
    )(x1, mod_in, g2, w1, w3, w2)


def _moe_body(se_ref, row0_ref, nsub_ref, nreal_ref, usrc_ref, tl0_ref, tln_ref,
              lp_ref, ls_ref, w1_ref, w3_ref, w2_ref, yp_ref, ys_ref,
              xbuf, acc, sem_in, sem_out, *, units_p, zero_unit, n_rt):
    g = pl.program_id(0)
    f = pl.program_id(1)
    n = nsub_ref[g]
    u0 = row0_ref[g] // UNIT
    per_sub = SUBM // UNIT
    batch = 4

    def rows(unit):
        return pl.ds(pl.multiple_of(unit * UNIT, UNIT), UNIT)

    def on_owner(unit, fn):
        @pl.when(unit < units_p)
        def _():
            fn(0, unit)

        @pl.when(unit >= units_p)
        def _():
            fn(1, unit - units_p)

    def y_copy(owner, unit, src):
        return pltpu.make_async_copy(src, (yp_ref, ys_ref)[owner].at[rows(unit)], sem_out)

    def x_copy(owner, unit, u, sem):
        return pltpu.make_async_copy((lp_ref, ls_ref)[owner].at[rows(unit)], xbuf.at[rows(u)], sem)

    def wait_rows(sem, count, m):
        def one(i, c):
            pltpu.make_async_copy(lp_ref.at[pl.ds(0, m)], xbuf.at[pl.ds(0, m)], sem).wait()
            return c
        lax.fori_loop(0, count, one, 0)

    @pl.when((g == 0) & (f == 0))
    def _():
        acc[0:UNIT, :] = jnp.zeros((UNIT, D), F32)
        for phase in ("start", "wait"):
            def tile_tail(i, c):
                def one(k, c2):
                    on_owner(tl0_ref[i] + k,
                             lambda w, unit: getattr(y_copy(w, unit, acc.at[pl.ds(0, UNIT)]), phase)())
                    return c2
                lax.fori_loop(0, tln_ref[i], one, 0)
                return c
            lax.fori_loop(0, n_rt, tile_tail, 0)

    @pl.when((f == 0) & (n > 0))
    def _():
        def start(i, c):
            for k in range(batch):
                u = i * batch + k
                sem = sem_in.at[lax.shift_right_logical(u, per_sub.bit_length() - 1)]
                on_owner(usrc_ref[u0 + u], lambda w, unit: x_copy(w, unit, u, sem).start(priority=1))
            return c
        lax.fori_loop(0, n * (per_sub // batch), start, 0)

    def scatter_start(sub, k):
        def start(i, c):
            for j in range(batch):
                u = sub * per_sub + i * batch + j
                unit = usrc_ref[u0 + u]

                @pl.when(unit != zero_unit)
                def _():
                    on_owner(unit, lambda w, dst: y_copy(w, dst, acc.at[rows(u)]).start(priority=1))
            return c
        lax.fori_loop(0, k * (per_sub // batch), start, 0)

    def run(mode):
        w1b = w1_ref[...].astype(BF16)
        w3b = w3_ref[...].astype(BF16)
        w2b = w2_ref[...].astype(BF16)

        def step(sub, k):
            if mode == "first":
                for j in range(k):
                    wait_rows(sem_in.at[sub + j], 1, SUBM)
            r = sub * SUBM if isinstance(sub, int) else pl.multiple_of(sub * SUBM, SUBM)
            m = k * SUBM
            h = xbuf[pl.ds(r, m), :].astype(BF16)
            a = _dot(h, w1b)
            bb = _dot(h, w3b)
            part = _dot((jax.nn.silu(a) * bb).astype(BF16), w2b)
            if mode == "first":
                acc[pl.ds(r, m), :] = part
            else:
                acc[pl.ds(r, m), :] = acc[pl.ds(r, m), :] + part
            if mode == "last":
                scatter_start(sub, k)

        step(0, 1)

        def pair(i, c):
            step(1 + 2 * i, 2)
            return c
        lax.fori_loop(0, lax.shift_right_logical(n - 1, 1), pair, 0)

        @pl.when((n & 1) == 0)
        def _():
            step(n - 1, 1)

        if mode == "last":
            n_real = nreal_ref[g]
            n_big = lax.shift_right_logical(n_real, per_sub.bit_length() - 1)
            wait_rows(sem_out, n_big, SUBM)
            wait_rows(sem_out, n_real - n_big * per_sub, UNIT)

    @pl.when((f == 0) & (n > 0))
    def _():
        run("first")

    @pl.when((f > 0) & (f < NF - 1) & (n > 0))
    def _():
        run("mid")

    @pl.when((f == NF - 1) & (n > 0))
    def _():
        run("last")


def _moe_grouped(l_p, l_s, w1, w3, w2, plan):
    st_e, st_row0, st_n, st_real, usrc, tl0, tln = plan
    n_st = st_e.shape[0]
    n_rt = tl0.shape[0]
    units_p = l_p.shape[0] // UNIT
    n_y_s = l_s.shape[0] - UNIT

    def fidx(g, f, ns):
        return jnp.where(ns[g] > 0, f, NF - 1)

    grid_spec = pltpu.PrefetchScalarGridSpec(
        num_scalar_prefetch=len(plan),
        grid=(n_st, NF),
        in_specs=[
            pl.BlockSpec(memory_space=pl.ANY),
            pl.BlockSpec(memory_space=pl.ANY),
            pl.BlockSpec((None, D, TF), lambda g, f, se, r0, ns, *_: (se[g], 0, fidx(g, f, ns))),
            pl.BlockSpec((None, D, TF), lambda g, f, se, r0, ns, *_: (se[g], 0, fidx(g, f, ns))),
            pl.BlockSpec((None, TF, D), lambda g, f, se, r0, ns, *_: (se[g], fidx(g, f, ns), 0)),
        ],
        out_specs=[pl.BlockSpec(memory_space=pl.ANY), pl.BlockSpec(memory_space=pl.ANY)],
        scratch_shapes=[pltpu.VMEM((SPT * SUBM, D), F32), pltpu.VMEM((SPT * SUBM, D), F32),
                        pltpu.SemaphoreType.DMA((SPT,)), pltpu.SemaphoreType.DMA(())],
    )
    return pl.pallas_call(
        functools.partial(_moe_body, units_p=units_p, zero_unit=(l_p.shape[0] + n_y_s) // UNIT, n_rt=n_rt),
        out_shape=[jax.ShapeDtypeStruct(l_p.shape, F32), jax.ShapeDtypeStruct((n_y_s, D), F32)],
        grid_spec=grid_spec,
        compiler_params=pltpu.CompilerParams(
            dimension_semantics=("arbitrary", "arbitrary"), vmem_limit_bytes=VMEM_LIMIT,
            has_side_effects=True),
        name="moe_grouped",
    )(*plan, l_p, l_s, w1, w3, w2)


def _combine_body(rt_ref, x_ref, mod_ref, fg_ref, y_ref, o_ref, *, reps, final):
    tc = x_ref.shape[0]

    def column(k):
        return jnp.transpose(jnp.broadcast_to(rt_ref[k:k + 1, :], (LANE, tc)))

    lane_id = lax.broadcasted_iota(I32, (tc, LR), 1).astype(F32)
    yb = y_ref[...].astype(BF16)

    def pick(k):
        slot = jnp.concatenate([column(k)] * (LR // LANE), axis=1)
        onehot = jnp.where(lane_id == slot, 1.0, 0.0).astype(BF16)
        return _dot(onehot, yb)

    g1 = jnp.concatenate([column(2)] * (D // LANE), axis=1)
    g2 = jnp.concatenate([column(3)] * (D // LANE), axis=1)
    x2 = x_ref[...] + _mod_chunk(mod_ref, 5, reps) * (g1 * pick(0) + g2 * pick(1))
    o_ref[...] = _rmsn(x2) * fg_ref[...] if final else x2


def _combine(x1, mod, final_g, y_local, route, *, tok_per_mod, final):
    n_tok = x1.shape[0]
    tc = TR
    mod_in, mod_spec, reps = _mod_spec(mod, tc, tok_per_mod)
    return pl.pallas_call(
        functools.partial(_combine_body, reps=reps, final=final),
        out_shape=jax.ShapeDtypeStruct((n_tok, D), F32),
        grid=(n_tok // tc,),
        in_specs=[pl.BlockSpec((N_EXP, tc), lambda i: (0, i)),
                  pl.BlockSpec((tc, D), lambda i: (i, 0)),
                  mod_spec,
                  _const_spec((1, D)),
                  pl.BlockSpec((LR, D), lambda i: (i, 0))],
        out_specs=pl.BlockSpec((tc, D), lambda i: (i, 0)),
        compiler_params=pltpu.CompilerParams(
            dimension_semantics=("arbitrary",), vmem_limit_bytes=VMEM_LIMIT),
        name="moe_combine",
    )(route, x1, mod_in, final_g, y_local)


def _final_norm_body(x_ref, fg_ref, o_ref):
    o_ref[...] = _rmsn(x_ref[...]) * fg_ref[...]


def _final_norm(x, final_g, *, tm=512):
    n_tok = x.shape[0]
    return pl.pallas_call(
        _final_norm_body,
        out_shape=jax.ShapeDtypeStruct((n_tok, D), F32),
        grid=(n_tok // tm,),
        in_specs=[pl.BlockSpec((tm, D), lambda i: (i, 0)), _const_spec((1, D))],
        out_specs=pl.BlockSpec((tm, D), lambda i: (i, 0)),
        name="final_norm",
    )(x, final_g)


def _expert_plan(seg_n, seg_0):
    n_rt = seg_n.shape[0]
    seg_n = seg_n.astype(I32)
    seg_0 = seg_0.astype(I32)
    total = jnp.sum(seg_n, axis=0)
    nsub = (total + SUBM - 1) // SUBM
    off = (jnp.cumsum(nsub) - nsub) * SUBM
    dst = off[None, :] + jnp.cumsum(seg_n, axis=0) - seg_n
    src = jnp.arange(n_rt, dtype=I32)[:, None] * LR + seg_0
    s_start = dst.T.reshape(-1) // UNIT
    s_len = seg_n.T.reshape(-1) // UNIT
    s_src = src.T.reshape(-1) // UNIT
    max_sub = (n_rt * (TOP_K * TR + N_EXP * (UNIT - 1))) // SUBM + N_EXP
    zero_unit = n_rt * LR // UNIT
    take = lambda table, idx: jnp.sum(
        jnp.where(idx[:, None] == jnp.arange(table.shape[0], dtype=I32)[None, :], table[None, :], 0), axis=1)
    unit = jnp.arange(max_sub * SUBM // UNIT, dtype=I32)
    owner = jnp.sum((s_start[None, :] <= unit[:, None]).astype(I32), axis=1) - 1
    owner = jnp.clip(owner, 0, s_start.shape[0] - 1)
    k = unit - take(s_start, owner)
    real = (k >= 0) & (k < take(s_len, owner))
    usrc = jnp.where(real, take(s_src, owner) + k, zero_unit)
    nst = (nsub + SPT - 1) // SPT
    n_st = (max_sub + N_EXP * (SPT - 1)) // SPT
    st_end = jnp.cumsum(nst)
    g = jnp.arange(n_st, dtype=I32)
    n_used = st_end[-1]
    valid = g < n_used
    which = lambda q: jnp.minimum(jnp.sum((st_end[None, :] <= q[:, None]).astype(I32), axis=1), N_EXP - 1)
    e = jnp.where(valid, which(g), which((n_used - 1).reshape(1)))
    nst_e, nsub_e = take(nst, e), take(nsub, e)
    kk = g - (take(st_end, e) - nst_e)
    base = nsub_e // jnp.maximum(nst_e, 1)
    rem = nsub_e - base * nst_e
    size = jnp.where(valid, base + (kk < rem).astype(I32), 0)
    row0 = jnp.where(valid, take(off, e) + (kk * base + jnp.minimum(kk, rem)) * SUBM, 0)
    first = row0 // UNIT
    inside = (unit[None, :] >= first[:, None]) & (unit[None, :] < (first + size * (SUBM // UNIT))[:, None])
    n_real = jnp.sum((inside & real[None, :]).astype(I32), axis=1)
    used = jnp.sum(seg_n, axis=1)
    tail0 = (jnp.arange(n_rt, dtype=I32) * LR + used) // UNIT
    tailn = (LR - used) // UNIT
    as_i32 = lambda *xs: tuple(x.astype(I32) for x in xs)
    return as_i32(e, row0, size, n_real, usrc, tail0, tailn)


def _layer_weights(l, sample, norm1_g, norm2_g, w_in, v_norm_g, v_norm_b, sgu_w, sgu_b,
                   pool_w, pool_scale, branch_a_g, branch_b_g, w_out, n_pos):
    row = lambda a: a[l].reshape(1, -1)
    if sample:
        w8 = jnp.transpose(sgu_w[l][:, :n_pos, :n_pos], (1, 2, 0))
        sw = jnp.repeat(w8, HEAD_D, axis=2).reshape(n_pos * n_pos, SGU_W)
        sb = jnp.repeat(sgu_b[l][:, :n_pos].T, HEAD_D, axis=1)
    else:
        sw = sgu_w[l]
        sb = jnp.repeat(sgu_b[l].T, HEAD_D, axis=1)
    pw = pool_w[l].astype(BF16)
    z = jnp.zeros((LANE, LANE), BF16)
    pw2 = jnp.stack([jnp.block([[pw[0], z], [z, pw[1]]]), jnp.block([[pw[2], z], [z, pw[3]]])])
    return [row(norm1_g), row(norm2_g), w_in[l].astype(BF16), row(v_norm_g), row(v_norm_b), sw, sb,
            pw2, row(pool_scale), row(branch_a_g), row(branch_b_g), w_out[l].astype(BF16)]


def kernel(x_prompt, x_sample, state_pool, c_prompt, c_sample, norm1_g, norm2_g, ada_w, ada_b, w_in, v_norm_g, v_norm_b, sgu_w, sgu_b, pool_w, pool_scale, branch_a_g, branch_b_g, w_out, ffn_w1, ffn_w3, ffn_w2, router_w, moe_w1, moe_w3, moe_w2, final_g):
    n_b, seq, _ = x_prompt.shape
    n_seq, n_pos, _ = x_sample.shape
    depth = ada_w.shape[0]
    n_p = n_b * seq
    n_s = n_seq * n_pos

    mod = _adaln(jnp.concatenate([c_prompt, c_sample], axis=0), ada_w, ada_b)
    mod_p, mod_s = mod[:, :n_b], mod[:, n_b:]

    xp = x_prompt.reshape(n_p, D)
    xs = jnp.swapaxes(x_sample, 0, 1).reshape(n_s, D)
    bufs = jnp.transpose(state_pool, (0, 2, 1, 3))
    fg = final_g.reshape(1, D)
    per_layer = (norm1_g, norm2_g, w_in, v_norm_g, v_norm_b, sgu_w, sgu_b, pool_w, pool_scale,
                 branch_a_g, branch_b_g, w_out)

    pool_p, pool_s, cv_p, cv_s = [], [], [], []
    normed = False
    for l in range(depth):
        lw_p = _layer_weights(l, False, *per_layer, n_pos)
        lw_s = _layer_weights(l, True, *per_layer, n_pos)
        g2 = norm2_g[l].reshape(1, D)
        if l % 2 == 0:
            x1p, pp, vp = _mixer_prompt(xp, mod_p[l], lw_p)
            x1s, ps, vs = _mixer_sample(xs, mod_s[l], bufs[l], lw_s)
            w1, w3, w2 = (w[l // 2].astype(BF16) for w in (ffn_w1, ffn_w3, ffn_w2))
            xp = _ffn(x1p, mod_p[l], g2, w1, w3, w2, tm=512, tok_per_mod=seq)
            xs = _ffn(x1s, mod_s[l], g2, w1, w3, w2, tm=2 * n_seq, tok_per_mod=0)
            normed = False
        else:
            m = l // 2
            rw = jnp.zeros((2 * N_EXP, D), BF16).at[:N_EXP].set(router_w[m].T.astype(BF16))
            x1p, l_p, rt_p, segn_p, seg0_p, pp, vp = _mixer_prompt(xp, mod_p[l], lw_p, route_w=rw)
            x1s, l_s, rt_s, segn_s, seg0_s, ps, vs = _mixer_sample(xs, mod_s[l], bufs[l], lw_s, route_w=rw)
            plan = _expert_plan(jnp.concatenate([segn_p[:, :, 0], segn_s[:, :, 0]], axis=0),
                                jnp.concatenate([seg0_p[:, :, 0], seg0_s[:, :, 0]], axis=0))
            y_p, y_s = _moe_grouped(l_p, l_s, moe_w1[m], moe_w3[m], moe_w2[m], plan)
            normed = l == depth - 1
            xp = _combine(x1p, mod_p[l], fg, y_p, rt_p, tok_per_mod=seq, final=normed)
            xs = _combine(x1s, mod_s[l], fg, y_s, rt_s, tok_per_mod=0, final=normed)
        pool_p.append(pp[:, HALO - POOL_BUF:])
        pool_s.append(jnp.swapaxes(ps, 0, 1))
        cv_p.append(vp)
        cv_s.append(jnp.swapaxes(vs, 0, 1))

    if not normed:
        xp = _final_norm(xp, fg)
        xs = _final_norm(xs, fg)
    y_prompt = xp.reshape(n_b, seq, D)
    y_sample = jnp.swapaxes(xs.reshape(n_pos, n_seq, D), 0, 1)
    return (y_prompt, y_sample, jnp.stack(pool_p), jnp.stack(pool_s), jnp.stack(cv_p), jnp.stack(cv_s))
```

```python
import functools

import jax
import jax.numpy as jnp
from jax import lax
from jax.experimental import pallas as pl
from jax.experimental.pallas import tpu as pltpu

F32 = jnp.float32
BF16 = jnp.bfloat16
I32 = jnp.int32

D = 1024
SGU_W = 512
POOL_W = 512
HEADS = 4
HEAD_D = 128
CHUNK = 128
WINDOWS = (2, 4, 8, 16)
POOL_BUF = 15
HALO = 16
IN_W = 2 * SGU_W + POOL_W
D_FF = 3584
N_EXP = 8
TOP_K = 2
EPS = 1e-6
PAST_LEN = 16384

LANE = 128
SUBLANE = 8
TF = 512
NF = D_FF // TF
TR = 512
UNIT = SUBLANE
LR = -(-(TOP_K * TR + N_EXP * (UNIT - 1)) // LANE) * LANE
SUBM = 256
SPT = 8
N_TRASH = 128
VMEM_LIMIT = 56 * 1024 * 1024


def _rmsn(x):
    return x * lax.rsqrt(jnp.mean(x * x, axis=-1, keepdims=True) + EPS)


def _expand(m, reps):
    return m if reps == 1 else jnp.concatenate([m] * reps, axis=0)


def _mod_chunk(mod_ref, i, reps):
    return _expand(mod_ref[:, i * D:(i + 1) * D], reps)


def _dot(a, b):
    return jnp.dot(a, b, preferred_element_type=F32)


def _adaln_body(c_ref, w_ref, b_ref, o_ref):
    s = jax.nn.silu(c_ref[...]).astype(BF16)
    o_ref[...] = _dot(s, w_ref[...].astype(BF16)) + b_ref[...]


def _adaln(c_all, ada_w, ada_b):
    depth, _, n_out = ada_w.shape
    rows = c_all.shape[0]
    tn = 1024
    return pl.pallas_call(
        _adaln_body,
        out_shape=jax.ShapeDtypeStruct((depth, rows, n_out), F32),
        grid=(depth, n_out // tn),
        in_specs=[
            pl.BlockSpec((rows, D), lambda l, n: (0, 0)),
            pl.BlockSpec((None, D, tn), lambda l, n: (l, 0, n)),
            pl.BlockSpec((None, 1, tn), lambda l, n: (l, 0, n)),
        ],
        out_specs=pl.BlockSpec((None, rows, tn), lambda l, n: (l, 0, n)),
        compiler_params=pltpu.CompilerParams(
            dimension_semantics=("arbitrary", "arbitrary"), vmem_limit_bytes=VMEM_LIMIT),
        name="adaln",
    )(c_all, ada_w, ada_b.reshape(depth, 1, n_out))


def _mixer_front(x, mod_ref, g1_ref, win_ref, vg_ref, vb_ref, reps):
    sh1 = _mod_chunk(mod_ref, 0, reps)
    sc1 = _mod_chunk(mod_ref, 1, reps)
    h = _rmsn(x) * g1_ref[...] * (1.0 + sc1) + sh1
    z = _dot(h.astype(BF16), win_ref[...])
    u = jax.nn.gelu(z[:, :SGU_W])
    vr = jax.nn.gelu(z[:, SGU_W:2 * SGU_W])
    p = z[:, 2 * SGU_W:]
    vs = []
    for hh in range(HEADS):
        vh = vr[:, hh * HEAD_D:(hh + 1) * HEAD_D]
        dlt = vh - jnp.mean(vh, axis=-1, keepdims=True)
        var = jnp.mean(dlt * dlt, axis=-1, keepdims=True)
        vs.append(dlt * lax.rsqrt(var + EPS))
    v = jnp.concatenate(vs, axis=-1) * vg_ref[...] + vb_ref[...]
    return u, v, p


def _mixer_back(x, ya, d, mod_ref, g2_ref, pw_ref, ps_ref, ga_ref, gb_ref, wout_ref, reps):
    db = d.astype(BF16)
    yb = jnp.concatenate(
        [_dot(db[:, :2 * LANE], pw_ref[0]), _dot(db[:, 2 * LANE:], pw_ref[1])], axis=-1) * ps_ref[...]
    mixin = jnp.concatenate([_rmsn(ya) * ga_ref[...], _rmsn(yb) * gb_ref[...]], axis=-1)
    mix = _dot(mixin.astype(BF16), wout_ref[...])
    x1 = x + _mod_chunk(mod_ref, 2, reps) * mix
    sh2 = _mod_chunk(mod_ref, 3, reps)
    sc2 = _mod_chunk(mod_ref, 4, reps)
    h2 = _rmsn(x1) * g2_ref[...] * (1.0 + sc2) + sh2
    return x1, h2


def _route_sort(h2, rw_ref, u_ref):
    t = h2.shape[0]
    hb = h2.astype(BF16)
    logits = lax.dot_general(rw_ref[...], hb, (((1,), (1,)), ((), ())),
                             preferred_element_type=F32)[:N_EXP]
    sub = lax.broadcasted_iota(I32, (N_EXP, t), 0).astype(F32)
    m1 = jnp.max(logits, axis=0, keepdims=True)
    i1 = jnp.min(jnp.where(logits == m1, sub, float(N_EXP)), axis=0, keepdims=True)
    rest = jnp.where(sub == i1, -jnp.inf, logits)
    m2 = jnp.max(rest, axis=0, keepdims=True)
    i2 = jnp.min(jnp.where(rest == m2, sub, float(N_EXP)), axis=0, keepdims=True)
    e2 = jnp.exp(m2 - m1)
    den = 1.0 + e2
    g1 = 1.0 / den
    g2 = e2 / den
    sub16 = lax.broadcasted_iota(I32, (2 * N_EXP, t), 0).astype(F32)
    mask16 = jnp.where((sub16 == i1) | (sub16 == i2), 1.0, 0.0)
    rank = _dot(mask16.astype(BF16), u_ref[...])[:N_EXP]
    cnt = jnp.sum(mask16[:N_EXP], axis=1, keepdims=True)
    cnt_pad = jnp.floor((cnt + float(UNIT - 1)) * (1.0 / UNIT)) * float(UNIT)
    sub1 = sub[:, 0:1]
    lo = jnp.zeros((N_EXP, 1), F32)
    for e in range(N_EXP - 1):
        lo = lo + jnp.where(sub1 > float(e), cnt_pad[e:e + 1, :], 0.0)
    base = lo + rank
    ls1 = jnp.sum(jnp.where(sub == i1, base, 0.0), axis=0, keepdims=True)
    ls2 = jnp.sum(jnp.where(sub == i2, base, 0.0), axis=0, keepdims=True)
    srow = lax.broadcasted_iota(I32, (LR, t), 0).astype(F32)
    perm = jnp.where((srow == ls1) | (srow == ls2), 1.0, 0.0).astype(BF16)
    local = _dot(perm, hb)
    zf = jnp.zeros((N_EXP, t), F32)
    route = jnp.where(sub == 0.0, ls1, jnp.where(sub == 1.0, ls2, jnp.where(
        sub == 2.0, g1, jnp.where(sub == 3.0, g2, zf))))
    return local, route, cnt_pad, lo


def _mixer_prompt_body(*refs, T, n_tiles, n_rt, emit_route):
    (x_ref, mod_ref, g1_ref, g2_ref, win_ref, vg_ref, vb_ref, sw_ref, sb_ref,
     pw_ref, ps_ref, ga_ref, gb_ref, wout_ref) = refs[:14]
    k = 14
    if emit_route:
        rw_ref, u_ref = refs[k:k + 2]
        k += 2
    x1_ref = refs[k]
    k += 1
    if emit_route:
        l_ref, rt_ref, segn_ref, seg0_ref = refs[k:k + 4]
        k += 4
    pstate_ref, cv_ref, pbuf = refs[k:k + 3]

    i = pl.program_id(0)
    j = lax.rem(i, n_tiles)

    def tile():
        x = x_ref[...]
        u, v, p = _mixer_front(x, mod_ref, g1_ref, win_ref, vg_ref, vb_ref, 1)

        nc = T // CHUNK
        row = lax.broadcasted_iota(I32, (CHUNK, CHUNK), 0)
        col = lax.broadcasted_iota(I32, (CHUNK, CHUNK), 1)
        vb16 = v.astype(BF16)
        heads = []
        for hh in range(HEADS):
            w = jnp.where(row >= col, sw_ref[hh], 0.0).astype(BF16)
            vcat = jnp.concatenate(
                [vb16[c * CHUNK:(c + 1) * CHUNK, hh * HEAD_D:(hh + 1) * HEAD_D] for c in range(nc)], axis=1)
            heads.append(_dot(w, vcat))
        mixed = jnp.concatenate(
            [jnp.concatenate([heads[hh][:, c * HEAD_D:(c + 1) * HEAD_D] for hh in range(HEADS)], axis=1)
             for c in range(nc)], axis=0)
        ya = u * (mixed + _expand(sb_ref[...], nc))

        @pl.when(j == 0)
        def _():
            pbuf[0:HALO, :] = jnp.zeros((HALO, POOL_W), F32)

        pbuf[HALO:HALO + T, :] = p
        run = pbuf[...]
        width = 1
        sums = []
        for win in WINDOWS:
            while width < win:
                run = run + pltpu.roll(run, width, 0)
                width *= 2
            assert width == win and width - 1 <= HALO
            sums.append(run[HALO:, :LANE])
            run = run[:, LANE:]
        pos = j * T + lax.broadcasted_iota(I32, (T, 1), 0)
        ds = []
        for g, win in enumerate(WINDOWS):
            cnt = jnp.minimum(win, pos + 1).astype(F32)
            ds.append(sums[g] / cnt - p[:, g * LANE:(g + 1) * LANE])
        d = jnp.concatenate(ds, axis=-1)
        pbuf[0:HALO, :] = p[T - HALO:, :]

        x1, h2 = _mixer_back(x, ya, d, mod_ref, g2_ref, pw_ref, ps_ref, ga_ref, gb_ref, wout_ref, 1)
        x1_ref[...] = x1

        @pl.when(j == n_tiles - 1)
        def _():
            pstate_ref[...] = p[T - HALO:, :]
            cv_ref[...] = v[T - CHUNK:, :]

        if emit_route:
            local, route, seg_n, seg_0 = _route_sort(h2, rw_ref, u_ref)
            l_ref[...] = local
            rt_ref[...] = route
            segn_ref[...] = jnp.broadcast_to(seg_n, (N_EXP, LANE))
            seg0_ref[...] = jnp.broadcast_to(seg_0, (N_EXP, LANE))

    if emit_route:
        pl.when(i < n_rt)(tile)

        @pl.when(i >= n_rt)
        def _():
            l_ref[...] = jnp.zeros((LR, D), F32)
    else:
        tile()


def _mixer_sample_body(*refs, n_seq, n_pos, emit_route):
    (x_ref, mod_ref, buf_ref, g1_ref, g2_ref, win_ref, vg_ref, vb_ref, sw_ref, sb_ref,
     pw_ref, ps_ref, ga_ref, gb_ref, wout_ref) = refs[:15]
    k = 15
    if emit_route:
        rw_ref, u_ref = refs[k:k + 2]
        k += 3
    x1_ref = refs[k]
    k += 1
    if emit_route:
        l_ref, rt_ref, segn_ref, seg0_ref = refs[k:k + 4]
        k += 4
    pstate_ref, cv_ref = refs[k:k + 2]

    reps = n_pos
    x = x_ref[...]
    u, v, p = _mixer_front(x, mod_ref, g1_ref, win_ref, vg_ref, vb_ref, reps)

    def slab(a, t):
        return a[t * n_seq:(t + 1) * n_seq, :]

    mixed = []
    for t in range(n_pos):
        acc = sb_ref[t:t + 1, :] + sw_ref[t * n_pos:t * n_pos + 1, :] * slab(v, 0)
        for s in range(1, t + 1):
            acc = acc + sw_ref[t * n_pos + s:t * n_pos + s + 1, :] * slab(v, s)
        mixed.append(acc)
    ya = u * jnp.concatenate(mixed, axis=0)

    hist = [buf_ref[jj] for jj in range(POOL_BUF)] + [slab(p, t) for t in range(n_pos)]
    drows = []
    for t in range(n_pos):
        parts = []
        for g, win in enumerate(WINDOWS):
            lo = g * LANE
            ws = hist[POOL_BUF + t][:, lo:lo + LANE]
            for kk in range(1, win):
                ws = ws + hist[POOL_BUF + t - kk][:, lo:lo + LANE]
            cnt = float(min(win, PAST_LEN + t + 1))
            parts.append(ws / cnt - hist[POOL_BUF + t][:, lo:lo + LANE])
        drows.append(jnp.concatenate(parts, axis=-1))
    d = jnp.concatenate(drows, axis=0)
    for jj in range(POOL_BUF):
        pstate_ref[jj] = hist[n_pos + jj]
    for t in range(n_pos):
        cv_ref[t] = slab(v, t)

    x1, h2 = _mixer_back(x, ya, d, mod_ref, g2_ref, pw_ref, ps_ref, ga_ref, gb_ref, wout_ref, reps)
    x1_ref[...] = x1

    if emit_route:
        for i in range((n_seq * n_pos) // TR):
            local, route, seg_n, seg_0 = _route_sort(h2[i * TR:(i + 1) * TR, :], rw_ref, u_ref)
            l_ref[i * LR:(i + 1) * LR, :] = local
            rt_ref[:, i * TR:(i + 1) * TR] = route
            segn_ref[i] = jnp.broadcast_to(seg_n, (N_EXP, LANE))
            seg0_ref[i] = jnp.broadcast_to(seg_0, (N_EXP, LANE))


def _const_spec(shape):
    nd = len(shape)
    return pl.BlockSpec(shape, lambda *_: (0,) * nd)


def _rank_matrix():
    return jnp.triu(jnp.ones((TR, TR), BF16), 1)


def _mixer_prompt(x, mod, lw, route_w=None, n_rt_other=0):
    T = TR
    n_b = mod.shape[0]
    n_tok = x.shape[0]
    seq = n_tok // n_b
    n_tiles = seq // T
    n_rt = n_tok // T
    emit_route = route_w is not None
    n_steps = n_rt + n_rt_other + 1 if emit_route else n_rt
    tile = lambda i: jnp.minimum(i, n_rt - 1)
    in_specs = [
        pl.BlockSpec((T, D), lambda i: (tile(i), 0)),
        pl.BlockSpec((None, 1, 6 * D), lambda i: (tile(i) // n_tiles, 0, 0)),
    ] + [_const_spec(w.shape) for w in lw]
    args = [x, mod.reshape(n_b, 1, 6 * D)] + list(lw)
    out_shape = [jax.ShapeDtypeStruct((n_tok, D), F32)]
    out_specs = [pl.BlockSpec((T, D), lambda i: (tile(i), 0))]
    if emit_route:
        umat = _rank_matrix()
        in_specs += [_const_spec(route_w.shape), _const_spec(umat.shape)]
        args += [route_w, umat]
        out_shape += [jax.ShapeDtypeStruct((n_steps * LR, D), F32),
                      jax.ShapeDtypeStruct((N_EXP, n_tok), F32),
                      jax.ShapeDtypeStruct((n_rt, N_EXP, LANE), F32),
                      jax.ShapeDtypeStruct((n_rt, N_EXP, LANE), F32)]
        out_specs += [pl.BlockSpec((LR, D), lambda i: (i, 0)),
                      pl.BlockSpec((N_EXP, T), lambda i: (0, tile(i))),
                      pl.BlockSpec((None, N_EXP, LANE), lambda i: (tile(i), 0, 0)),
                      pl.BlockSpec((None, N_EXP, LANE), lambda i: (tile(i), 0, 0))]
    out_shape += [jax.ShapeDtypeStruct((n_b, HALO, POOL_W), F32),
                  jax.ShapeDtypeStruct((n_b, CHUNK, SGU_W), F32)]
    out_specs += [pl.BlockSpec((None, HALO, POOL_W), lambda i: (tile(i) // n_tiles, 0, 0)),
                  pl.BlockSpec((None, CHUNK, SGU_W), lambda i: (tile(i) // n_tiles, 0, 0))]
    return pl.pallas_call(
        functools.partial(_mixer_prompt_body, T=T, n_tiles=n_tiles, n_rt=n_rt, emit_route=emit_route),
        out_shape=out_shape,
        grid=(n_steps,),
        in_specs=in_specs,
        out_specs=out_specs,
        scratch_shapes=[pltpu.VMEM((HALO + T, POOL_W), F32)],
        compiler_params=pltpu.CompilerParams(
            dimension_semantics=("arbitrary",), vmem_limit_bytes=VMEM_LIMIT),
        name="mixer_prompt_route" if emit_route else "mixer_prompt",
    )(*args)


def _mixer_sample(x, mod, buf, lw, route_w=None, l_all=None, blk0=0):
    n_seq = mod.shape[0]
    n_tok = x.shape[0]
    n_pos = n_tok // n_seq
    n_rt = n_tok // TR
    emit_route = route_w is not None
    in_specs = [_const_spec(x.shape), _const_spec(mod.shape), _const_spec(buf.shape)]
    in_specs += [_const_spec(w.shape) for w in lw]
    args = [x, mod, buf] + list(lw)
    out_shape = [jax.ShapeDtypeStruct((n_tok, D), F32)]
    out_specs = [_const_spec((n_tok, D))]
    aliases = {}
    if emit_route:
        umat = _rank_matrix()
        in_specs += [_const_spec(route_w.shape), _const_spec(umat.shape), pl.BlockSpec(memory_space=pl.ANY)]
        args += [route_w, umat, l_all]
        aliases = {len(args) - 1: 1}
        assert blk0 % n_rt == 0
        out_shape += [jax.ShapeDtypeStruct(l_all.shape, F32),
                      jax.ShapeDtypeStruct((N_EXP, n_tok), F32),
                      jax.ShapeDtypeStruct((n_rt, N_EXP, LANE), F32),
                      jax.ShapeDtypeStruct((n_rt, N_EXP, LANE), F32)]
        out_specs += [pl.BlockSpec((n_rt * LR, D), lambda i: (blk0 // n_rt, 0)),
                      _const_spec((N_EXP, n_tok)),
                      _const_spec((n_rt, N_EXP, LANE)), _const_spec((n_rt, N_EXP, LANE))]
    out_shape += [jax.ShapeDtypeStruct((POOL_BUF, n_seq, POOL_W), F32),
                  jax.ShapeDtypeStruct((n_pos, n_seq, SGU_W), F32)]
    out_specs += [_const_spec((POOL_BUF, n_seq, POOL_W)), _const_spec((n_pos, n_seq, SGU_W))]
    return pl.pallas_call(
        functools.partial(_mixer_sample_body, n_seq=n_seq, n_pos=n_pos, emit_route=emit_route),
        out_shape=out_shape,
        grid=(1,),
        in_specs=in_specs,
        out_specs=out_specs,
        input_output_aliases=aliases,
        compiler_params=pltpu.CompilerParams(
            dimension_semantics=("arbitrary",), vmem_limit_bytes=VMEM_LIMIT),
        name="mixer_sample_route" if emit_route else "mixer_sample",
    )(*args)


def _ffn_body(x_ref, mod_ref, g2_ref, w1_ref, w3_ref, w2_ref, o_ref, *, reps):
    x1 = x_ref[...]
    sh2 = _mod_chunk(mod_ref, 3, reps)
    sc2 = _mod_chunk(mod_ref, 4, reps)
    h = (_rmsn(x1) * g2_ref[...] * (1.0 + sc2) + sh2).astype(BF16)
    acc = None
    for c in range(NF):
        a = _dot(h, w1_ref[:, c * TF:(c + 1) * TF])
        bb = _dot(h, w3_ref[:, c * TF:(c + 1) * TF])
        part = _dot((jax.nn.silu(a) * bb).astype(BF16), w2_ref[c * TF:(c + 1) * TF, :])
        acc = part if acc is None else acc + part
    o_ref[...] = x1 + _mod_chunk(mod_ref, 5, reps) * acc


def _mod_spec(mod, tile_rows, tok_per_mod):
    if tok_per_mod:
        spec = pl.BlockSpec((None, 1, 6 * D), lambda i, *_: ((i * tile_rows) // tok_per_mod, 0, 0))
        return mod.reshape(mod.shape[0], 1, 6 * D), spec, 1
    return mod, pl.BlockSpec(mod.shape, lambda i, *_: (0, 0)), tile_rows // mod.shape[0]


def _ffn(x1, mod, g2, w1, w3, w2, *, tm, tok_per_mod):
    n_tok = x1.shape[0]
    mod_in, mod_spec, reps = _mod_spec(mod, tm, tok_per_mod)
    resident = lambda s: pl.BlockSpec(s, lambda i: (0,) * len(s), pipeline_mode=pl.Buffered(1))
    return pl.pallas_call(
        functools.partial(_ffn_body, reps=reps),
        out_shape=jax.ShapeDtypeStruct((n_tok, D), F32),
        grid=(n_tok // tm,),
        in_specs=[pl.BlockSpec((tm, D), lambda i: (i, 0)), mod_spec, _const_spec(g2.shape),
                  resident(w1.shape), resident(w3.shape), resident(w2.shape)],
        out_specs=pl.BlockSpec((tm, D), lambda i: (i, 0)),
        compiler_params=pltpu.CompilerParams(
            dimension_semantics=("arbitrary",), vmem_limit_bytes=VMEM_LIMIT),
        name="ffn_dense",
    )(x1, mod_in, g2, w1, w3, w2)


def _moe_body(se_ref, row0_ref, nsub_ref, uin_ref, uout_ref, tl0_ref, tln_ref,
              l_ref, w1_ref, w3_ref, w2_ref, y_ref, xbuf, acc, sem_in, sem_out, *, n_tail):
    g = pl.program_id(0)
    f = pl.program_id(1)
    n = nsub_ref[g]
    u0 = row0_ref[g] // UNIT
    per_sub = SUBM // UNIT
    sub_shift = per_sub.bit_length() - 1
    batch = 4

    def rows(unit):
        return pl.ds(pl.multiple_of(unit * UNIT, UNIT), UNIT)

    def x_copy(unit, u, sem):
        return pltpu.make_async_copy(l_ref.at[rows(unit)], xbuf.at[rows(u)], sem)

    def y_copy(unit, src):
        return pltpu.make_async_copy(src, y_ref.at[rows(unit)], sem_out)

    def wait_rows(sem, count, m):
        def one(i, c):
            pltpu.make_async_copy(l_ref.at[pl.ds(0, m)], xbuf.at[pl.ds(0, m)], sem).wait()
            return c
        lax.fori_loop(0, count, one, 0)

    @pl.when((g == 0) & (f == 0))
    def _():
        acc[0:UNIT, :] = jnp.zeros((UNIT, D), F32)
        for phase in ("start", "wait"):
            def tile_tail(i, c):
                def one(k, c2):
                    getattr(y_copy(tl0_ref[i] + k, acc.at[pl.ds(0, UNIT)]), phase)()
                    return c2
                lax.fori_loop(0, tln_ref[i], one, 0)
                return c
            lax.fori_loop(0, n_tail, tile_tail, 0)

    @pl.when((f == 0) & (n > 0))
    def _():
        def start(i, c):
            for k in range(batch):
                u = i * batch + k
                x_copy(uin_ref[u0 + u], u, sem_in.at[lax.shift_right_logical(u, sub_shift)]).start(priority=1)
            return c
        lax.fori_loop(0, n * (per_sub // batch), start, 0)

    def scatter_start(sub, k):
        def start(i, c):
            for j in range(batch):
                u = sub * per_sub + i * batch + j
                y_copy(uout_ref[u0 + u], acc.at[rows(u)]).start(priority=1)
            return c
        lax.fori_loop(0, k * (per_sub // batch), start, 0)

    def run(mode):
        w1b = w1_ref[...].astype(BF16)
        w3b = w3_ref[...].astype(BF16)
        w2b = w2_ref[...].astype(BF16)

        def step(sub, k):
            if mode == "first":
                for j in range(k):
                    wait_rows(sem_in.at[sub + j], 1, SUBM)
            r = sub * SUBM if isinstance(sub, int) else pl.multiple_of(sub * SUBM, SUBM)
            m = k * SUBM
            h = xbuf[pl.ds(r, m), :].astype(BF16)
            a = _dot(h, w1b)
            bb = _dot(h, w3b)
            part = _dot((jax.nn.silu(a) * bb).astype(BF16), w2b)
            if mode == "first":
                acc[pl.ds(r, m), :] = part
            else:
                acc[pl.ds(r, m), :] = acc[pl.ds(r, m), :] + part
            if mode == "last":
                scatter_start(sub, k)

        def loop(count, first_sub, k):
            def body(i, c):
                step(first_sub + k * i, k)
                return c
            lax.fori_loop(0, count, body, 0)

        step(0, 1)
        rest = n - 1
        big = 4 if mode == "mid" else 2
        n_big = lax.shift_right_logical(rest, big.bit_length() - 1)
        loop(n_big, 1, big)
        done = 1 + n_big * big
        k = big // 2
        while k >= 1:
            @pl.when((rest & k) != 0)
            def _(k=k, done=done):
                step(done, k)
            done = done + (rest & k)
            k //= 2

        if mode == "last":
            wait_rows(sem_out, n, SUBM)

    @pl.when((f == 0) & (n > 0))
    def _():
        run("first")

    @pl.when((f > 0) & (f < NF - 1) & (n > 0))
    def _():
        run("mid")

    @pl.when((f == NF - 1) & (n > 0))
    def _():
        run("last")


def _moe_grouped(l_all, w1, w3, w2, plan):
    st_e = plan[0]
    n_st = st_e.shape[0]
    n_tail = plan[-1].shape[0]

    def fidx(g, f, ns):
        return jnp.where(ns[g] > 0, f, NF - 1)

    grid_spec = pltpu.PrefetchScalarGridSpec(
        num_scalar_prefetch=len(plan),
        grid=(n_st, NF),
        in_specs=[
            pl.BlockSpec(memory_space=pl.ANY),
            pl.BlockSpec((None, D, TF), lambda g, f, se, r0, ns, *_: (se[g], 0, fidx(g, f, ns))),
            pl.BlockSpec((None, D, TF), lambda g, f, se, r0, ns, *_: (se[g], 0, fidx(g, f, ns))),
            pl.BlockSpec((None, TF, D), lambda g, f, se, r0, ns, *_: (se[g], fidx(g, f, ns), 0)),
        ],
        out_specs=pl.BlockSpec(memory_space=pl.ANY),
        scratch_shapes=[pltpu.VMEM((SPT * SUBM, D), F32), pltpu.VMEM((SPT * SUBM, D), F32),
                        pltpu.SemaphoreType.DMA((SPT,)), pltpu.SemaphoreType.DMA(())],
    )
    return pl.pallas_call(
        functools.partial(_moe_body, n_tail=n_tail),
        out_shape=jax.ShapeDtypeStruct(l_all.shape, F32),
        grid_spec=grid_spec,
        compiler_params=pltpu.CompilerParams(
            dimension_semantics=("arbitrary", "arbitrary"), vmem_limit_bytes=VMEM_LIMIT,
            has_side_effects=True),
        name="moe_grouped",
    )(*plan, l_all, w1, w3, w2)


def _combine_body(rt_ref, x_ref, mod_ref, fg_ref, y_ref, o_ref, *, reps, final):
    tc = x_ref.shape[0]

    def column(k):
        return jnp.transpose(jnp.broadcast_to(rt_ref[k:k + 1, :], (LANE, tc)))

    lane_id = lax.broadcasted_iota(I32, (tc, LR), 1).astype(F32)
    yb = y_ref[...].astype(BF16)

    def pick(k):
        slot = jnp.concatenate([column(k)] * (LR // LANE), axis=1)
        onehot = jnp.where(lane_id == slot, 1.0, 0.0).astype(BF16)
        return _dot(onehot, yb)

    g1 = jnp.concatenate([column(2)] * (D // LANE), axis=1)
    g2 = jnp.concatenate([column(3)] * (D // LANE), axis=1)
    x2 = x_ref[...] + _mod_chunk(mod_ref, 5, reps) * (g1 * pick(0) + g2 * pick(1))
    o_ref[...] = _rmsn(x2) * fg_ref[...] if final else x2


def _combine(x1, mod, final_g, y_all, route, *, blk0, tok_per_mod, final):
    n_tok = x1.shape[0]
    tc = TR
    mod_in, mod_spec, reps = _mod_spec(mod, tc, tok_per_mod)
    return pl.pallas_call(
        functools.partial(_combine_body, reps=reps, final=final),
        out_shape=jax.ShapeDtypeStruct((n_tok, D), F32),
        grid=(n_tok // tc,),
        in_specs=[pl.BlockSpec((N_EXP, tc), lambda i: (0, i)),
                  pl.BlockSpec((tc, D), lambda i: (i, 0)),
                  mod_spec,
                  _const_spec((1, D)),
                  pl.BlockSpec((LR, D), lambda i: (blk0 + i, 0))],
        out_specs=pl.BlockSpec((tc, D), lambda i: (i, 0)),
        compiler_params=pltpu.CompilerParams(
            dimension_semantics=("arbitrary",), vmem_limit_bytes=VMEM_LIMIT),
        name="moe_combine",
    )(route, x1, mod_in, final_g, y_all)


def _final_norm_body(x_ref, fg_ref, o_ref):
    o_ref[...] = _rmsn(x_ref[...]) * fg_ref[...]


def _final_norm(x, final_g, *, tm=512):
    n_tok = x.shape[0]
    return pl.pallas_call(
        _final_norm_body,
        out_shape=jax.ShapeDtypeStruct((n_tok, D), F32),
        grid=(n_tok // tm,),
        in_specs=[pl.BlockSpec((tm, D), lambda i: (i, 0)), _const_spec((1, D))],
        out_specs=pl.BlockSpec((tm, D), lambda i: (i, 0)),
        name="final_norm",
    )(x, final_g)


def _expert_plan(seg_n, seg_0):
    n_rt = seg_n.shape[0]
    seg_n = seg_n.astype(I32)
    seg_0 = seg_0.astype(I32)
    total = jnp.sum(seg_n, axis=0)
    nsub = (total + SUBM - 1) // SUBM
    off = (jnp.cumsum(nsub) - nsub) * SUBM
    dst = off[None, :] + jnp.cumsum(seg_n, axis=0) - seg_n
    src = jnp.arange(n_rt, dtype=I32)[:, None] * LR + seg_0
    s_start = dst.T.reshape(-1) // UNIT
    s_len = seg_n.T.reshape(-1) // UNIT
    s_src = src.T.reshape(-1) // UNIT
    max_sub = (n_rt * (TOP_K * TR + N_EXP * (UNIT - 1))) // SUBM + N_EXP
    zero_unit = n_rt * LR // UNIT
    take = lambda table, idx: jnp.sum(
        jnp.where(idx[:, None] == jnp.arange(table.shape[0], dtype=I32)[None, :], table[None, :], 0), axis=1)
    unit = jnp.arange(max_sub * SUBM // UNIT, dtype=I32)
    owner = jnp.sum((s_start[None, :] <= unit[:, None]).astype(I32), axis=1) - 1
    owner = jnp.clip(owner, 0, s_start.shape[0] - 1)
    k = unit - take(s_start, owner)
    real = (k >= 0) & (k < take(s_len, owner))
    u_src = take(s_src, owner) + k
    u_in = jnp.where(real, u_src, zero_unit)
    u_out = jnp.where(real, u_src, zero_unit + 1 + unit % N_TRASH)
    nst = (nsub + SPT - 1) // SPT
    n_st = (max_sub + N_EXP * (SPT - 1)) // SPT
    st_end = jnp.cumsum(nst)
    g = jnp.arange(n_st, dtype=I32)
    n_used = st_end[-1]
    valid = g < n_used
    which = lambda q: jnp.minimum(jnp.sum((st_end[None, :] <= q[:, None]).astype(I32), axis=1), N_EXP - 1)
    e = jnp.where(valid, which(g), which((n_used - 1).reshape(1)))
    nst_e, nsub_e = take(nst, e), take(nsub, e)
    kk = g - (take(st_end, e) - nst_e)
    base = nsub_e // jnp.maximum(nst_e, 1)
    rem = nsub_e - base * nst_e
    size = jnp.where(valid, base + (kk < rem).astype(I32), 0)
    row0 = jnp.where(valid, take(off, e) + (kk * base + jnp.minimum(kk, rem)) * SUBM, 0)
    used = jnp.concatenate([jnp.sum(seg_n, axis=1), jnp.zeros((1,), I32)])
    tail0 = (jnp.arange(n_rt + 1, dtype=I32) * LR + used) // UNIT
    tailn = (LR - used) // UNIT
    as_i32 = lambda *xs: tuple(x.astype(I32) for x in xs)
    return as_i32(e, row0, size, u_in, u_out, tail0, tailn)


def _layer_weights(l, sample, norm1_g, norm2_g, w_in, v_norm_g, v_norm_b, sgu_w, sgu_b,
                   pool_w, pool_scale, branch_a_g, branch_b_g, w_out, n_pos):
    row = lambda a: a[l].reshape(1, -1)
    if sample:
        w8 = jnp.transpose(sgu_w[l][:, :n_pos, :n_pos], (1, 2, 0))
        sw = jnp.repeat(w8, HEAD_D, axis=2).reshape(n_pos * n_pos, SGU_W)
        sb = jnp.repeat(sgu_b[l][:, :n_pos].T, HEAD_D, axis=1)
    else:
        sw = sgu_w[l]
        sb = jnp.repeat(sgu_b[l].T, HEAD_D, axis=1)
    pw = pool_w[l].astype(BF16)
    z = jnp.zeros((LANE, LANE), BF16)
    pw2 = jnp.stack([jnp.block([[pw[0], z], [z, pw[1]]]), jnp.block([[pw[2], z], [z, pw[3]]])])
    return [row(norm1_g), row(norm2_g), w_in[l].astype(BF16), row(v_norm_g), row(v_norm_b), sw, sb,
            pw2, row(pool_scale), row(branch_a_g), row(branch_b_g), w_out[l].astype(BF16)]


def kernel(x_prompt, x_sample, state_pool, c_prompt, c_sample, norm1_g, norm2_g, ada_w, ada_b, w_in, v_norm_g, v_norm_b, sgu_w, sgu_b, pool_w, pool_scale, branch_a_g, branch_b_g, w_out, ffn_w1, ffn_w3, ffn_w2, router_w, moe_w1, moe_w3, moe_w2, final_g):
    n_b, seq, _ = x_prompt.shape
    n_seq, n_pos, _ = x_sample.shape
    depth = ada_w.shape[0]
    n_p = n_b * seq
    n_s = n_seq * n_pos

    mod = _adaln(jnp.concatenate([c_prompt, c_sample], axis=0), ada_w, ada_b)
    mod_p, mod_s = mod[:, :n_b], mod[:, n_b:]

    xp = x_prompt.reshape(n_p, D)
    xs = jnp.swapaxes(x_sample, 0, 1).reshape(n_s, D)
    bufs = jnp.transpose(state_pool, (0, 2, 1, 3))
    fg = final_g.reshape(1, D)
    per_layer = (norm1_g, norm2_g, w_in, v_norm_g, v_norm_b, sgu_w, sgu_b, pool_w, pool_scale,
                 branch_a_g, branch_b_g, w_out)

    pool_p, pool_s, cv_p, cv_s = [], [], [], []
    normed = False
    for l in range(depth):
        lw_p = _layer_weights(l, False, *per_layer, n_pos)
        lw_s = _layer_weights(l, True, *per_layer, n_pos)
        g2 = norm2_g[l].reshape(1, D)
        if l % 2 == 0:
            x1p, pp, vp = _mixer_prompt(xp, mod_p[l], lw_p)
            x1s, ps, vs = _mixer_sample(xs, mod_s[l], bufs[l], lw_s)
            w1, w3, w2 = (w[l // 2].astype(BF16) for w in (ffn_w1, ffn_w3, ffn_w2))
            xp = _ffn(x1p, mod_p[l], g2, w1, w3, w2, tm=512, tok_per_mod=seq)
            xs = _ffn(x1s, mod_s[l], g2, w1, w3, w2, tm=2 * n_seq, tok_per_mod=0)
            normed = False
        else:
            m = l // 2
            rw = jnp.zeros((2 * N_EXP, D), BF16).at[:N_EXP].set(router_w[m].T.astype(BF16))
            rt_tiles_p, rt_tiles_s = n_p // TR, n_s // TR
            x1p, l_all, rt_p, segn_p, seg0_p, pp, vp = _mixer_prompt(
                xp, mod_p[l], lw_p, route_w=rw, n_rt_other=rt_tiles_s)
            x1s, l_all, rt_s, segn_s, seg0_s, ps, vs = _mixer_sample(
                xs, mod_s[l], bufs[l], lw_s, route_w=rw, l_all=l_all, blk0=rt_tiles_p)
            plan = _expert_plan(jnp.concatenate([segn_p[:, :, 0], segn_s[:, :, 0]], axis=0),
                                jnp.concatenate([seg0_p[:, :, 0], seg0_s[:, :, 0]], axis=0))
            y_all = _moe_grouped(l_all, moe_w1[m], moe_w3[m], moe_w2[m], plan)
            normed = l == depth - 1
            xp = _combine(x1p, mod_p[l], fg, y_all, rt_p, blk0=0, tok_per_mod=seq, final=normed)
            xs = _combine(x1s, mod_s[l], fg, y_all, rt_s, blk0=rt_tiles_p, tok_per_mod=0, final=normed)
        pool_p.append(pp[:, HALO - POOL_BUF:])
        pool_s.append(jnp.swapaxes(ps, 0, 1))
        cv_p.append(vp)
        cv_s.append(jnp.swapaxes(vs, 0, 1))

    if not normed:
        xp = _final_norm(xp, fg)
        xs = _final_norm(xs, fg)
    y_prompt = xp.reshape(n_b, seq, D)
    y_sample = jnp.swapaxes(xs.reshape(n_pos, n_seq, D), 0, 1)
    return (y_prompt, y_sample, jnp.stack(pool_p), jnp.stack(pool_s), jnp.stack(cv_p), jnp.stack(cv_s))
```

```python
import functools

import jax
import jax.numpy as jnp
from jax import lax
from jax.experimental import pallas as pl
from jax.experimental.pallas import tpu as pltpu

F32 = jnp.float32
BF16 = jnp.bfloat16
I32 = jnp.int32

D = 1024
SGU_W = 512
POOL_W = 512
HEADS = 4
HEAD_D = 128
CHUNK = 128
WINDOWS = (2, 4, 8, 16)
POOL_BUF = 15
HALO = 16
IN_W = 2 * SGU_W + POOL_W
D_FF = 3584
N_EXP = 8
TOP_K = 2
EPS = 1e-6
PAST_LEN = 16384

LANE = 128
SUBLANE = 8
TF = 512
NF = D_FF // TF
TR = 512
UNIT = SUBLANE
LR = -(-(TOP_K * TR + N_EXP * (UNIT - 1)) // LANE) * LANE
SUBM = 256
SPT = 9
N_TRASH = 128
VMEM_LIMIT = 56 * 1024 * 1024


def _rmsn(x):
    return x * lax.rsqrt(jnp.mean(x * x, axis=-1, keepdims=True) + EPS)


def _expand(m, reps):
    return m if reps == 1 else jnp.concatenate([m] * reps, axis=0)


def _mod_chunk(mod_ref, i, reps):
    return _expand(mod_ref[:, i * D:(i + 1) * D], reps)


def _dot(a, b):
    return jnp.dot(a, b, preferred_element_type=F32)


def _adaln_body(c_ref, w_ref, b_ref, o_ref):
    s = jax.nn.silu(c_ref[...]).astype(BF16)
    o_ref[...] = _dot(s, w_ref[...].astype(BF16)) + b_ref[...]


def _adaln(c_all, ada_w, ada_b):
    depth, _, n_out = ada_w.shape
    rows = c_all.shape[0]
    tn = 1024
    return pl.pallas_call(
        _adaln_body,
        out_shape=jax.ShapeDtypeStruct((depth, rows, n_out), F32),
        grid=(depth, n_out // tn),
        in_specs=[
            pl.BlockSpec((rows, D), lambda l, n: (0, 0)),
            pl.BlockSpec((None, D, tn), lambda l, n: (l, 0, n)),
            pl.BlockSpec((None, 1, tn), lambda l, n: (l, 0, n)),
        ],
        out_specs=pl.BlockSpec((None, rows, tn), lambda l, n: (l, 0, n)),
        compiler_params=pltpu.CompilerParams(
            dimension_semantics=("arbitrary", "arbitrary"), vmem_limit_bytes=VMEM_LIMIT),
        name="adaln",
    )(c_all, ada_w, ada_b.reshape(depth, 1, n_out))


def _mixer_front(x, mod_ref, g1_ref, win_ref, vg_ref, vb_ref, reps):
    sh1 = _mod_chunk(mod_ref, 0, reps)
    sc1 = _mod_chunk(mod_ref, 1, reps)
    h = _rmsn(x) * g1_ref[...] * (1.0 + sc1) + sh1
    z = _dot(h.astype(BF16), win_ref[...])
    u = jax.nn.gelu(z[:, :SGU_W])
    vr = jax.nn.gelu(z[:, SGU_W:2 * SGU_W])
    p = z[:, 2 * SGU_W:]
    vs = []
    for hh in range(HEADS):
        vh = vr[:, hh * HEAD_D:(hh + 1) * HEAD_D]
        dlt = vh - jnp.mean(vh, axis=-1, keepdims=True)
        var = jnp.mean(dlt * dlt, axis=-1, keepdims=True)
        vs.append(dlt * lax.rsqrt(var + EPS))
    v = jnp.concatenate(vs, axis=-1) * vg_ref[...] + vb_ref[...]
    return u, v, p


def _mixer_back(x, ya, d, mod_ref, g2_ref, pw_ref, ps_ref, ga_ref, gb_ref, wout_ref, reps):
    db = d.astype(BF16)
    yb = jnp.concatenate(
        [_dot(db[:, :2 * LANE], pw_ref[0]), _dot(db[:, 2 * LANE:], pw_ref[1])], axis=-1) * ps_ref[...]
    mixin = jnp.concatenate([_rmsn(ya) * ga_ref[...], _rmsn(yb) * gb_ref[...]], axis=-1)
    mix = _dot(mixin.astype(BF16), wout_ref[...])
    x1 = x + _mod_chunk(mod_ref, 2, reps) * mix
    sh2 = _mod_chunk(mod_ref, 3, reps)
    sc2 = _mod_chunk(mod_ref, 4, reps)
    h2 = _rmsn(x1) * g2_ref[...] * (1.0 + sc2) + sh2
    return x1, h2


def _route_sort(h2, rw_ref, u_ref):
    t = h2.shape[0]
    hb = h2.astype(BF16)
    logits = lax.dot_general(rw_ref[...], hb, (((1,), (1,)), ((), ())),
                             preferred_element_type=F32)[:N_EXP]
    sub = lax.broadcasted_iota(I32, (N_EXP, t), 0).astype(F32)
    m1 = jnp.max(logits, axis=0, keepdims=True)
    i1 = jnp.min(jnp.where(logits == m1, sub, float(N_EXP)), axis=0, keepdims=True)
    rest = jnp.where(sub == i1, -jnp.inf, logits)
    m2 = jnp.max(rest, axis=0, keepdims=True)
    i2 = jnp.min(jnp.where(rest == m2, sub, float(N_EXP)), axis=0, keepdims=True)
    e2 = jnp.exp(m2 - m1)
    den = 1.0 + e2
    g1 = 1.0 / den
    g2 = e2 / den
    sub16 = lax.broadcasted_iota(I32, (2 * N_EXP, t), 0).astype(F32)
    mask16 = jnp.where((sub16 == i1) | (sub16 == i2), 1.0, 0.0)
    rank = _dot(mask16.astype(BF16), u_ref[...])[:N_EXP]
    cnt = jnp.sum(mask16[:N_EXP], axis=1, keepdims=True)
    cnt_pad = jnp.floor((cnt + float(UNIT - 1)) * (1.0 / UNIT)) * float(UNIT)
    sub1 = sub[:, 0:1]
    lo = jnp.zeros((N_EXP, 1), F32)
    for e in range(N_EXP - 1):
        lo = lo + jnp.where(sub1 > float(e), cnt_pad[e:e + 1, :], 0.0)
    base = lo + rank
    ls1 = jnp.sum(jnp.where(sub == i1, base, 0.0), axis=0, keepdims=True)
    ls2 = jnp.sum(jnp.where(sub == i2, base, 0.0), axis=0, keepdims=True)
    srow = lax.broadcasted_iota(I32, (LR, t), 0).astype(F32)
    perm = jnp.where((srow == ls1) | (srow == ls2), 1.0, 0.0).astype(BF16)
    local = _dot(perm, hb)
    zf = jnp.zeros((N_EXP, t), F32)
    route = jnp.where(sub == 0.0, ls1, jnp.where(sub == 1.0, ls2, jnp.where(
        sub == 2.0, g1, jnp.where(sub == 3.0, g2, zf))))
    return local, route, cnt_pad, lo


def _mixer_prompt_body(*refs, T, n_tiles, n_rt, emit_route):
    (x_ref, mod_ref, g1_ref, g2_ref, win_ref, vg_ref, vb_ref, sw_ref, sb_ref,
     pw_ref, ps_ref, ga_ref, gb_ref, wout_ref) = refs[:14]
    k = 14
    if emit_route:
        rw_ref, u_ref = refs[k:k + 2]
        k += 2
    x1_ref = refs[k]
    k += 1
    if emit_route:
        l_ref, rt_ref, segn_ref, seg0_ref = refs[k:k + 4]
        k += 4
    pstate_ref, cv_ref, pbuf = refs[k:k + 3]

    i = pl.program_id(0)
    j = lax.rem(i, n_tiles)

    def tile():
        x = x_ref[...]
        u, v, p = _mixer_front(x, mod_ref, g1_ref, win_ref, vg_ref, vb_ref, 1)

        nc = T // CHUNK
        row = lax.broadcasted_iota(I32, (CHUNK, CHUNK), 0)
        col = lax.broadcasted_iota(I32, (CHUNK, CHUNK), 1)
        vb16 = v.astype(BF16)
        heads = []
        for hh in range(HEADS):
            w = jnp.where(row >= col, sw_ref[hh], 0.0).astype(BF16)
            vcat = jnp.concatenate(
                [vb16[c * CHUNK:(c + 1) * CHUNK, hh * HEAD_D:(hh + 1) * HEAD_D] for c in range(nc)], axis=1)
            heads.append(_dot(w, vcat))
        mixed = jnp.concatenate(
            [jnp.concatenate([heads[hh][:, c * HEAD_D:(c + 1) * HEAD_D] for hh in range(HEADS)], axis=1)
             for c in range(nc)], axis=0)
        ya = u * (mixed + _expand(sb_ref[...], nc))

        @pl.when(j == 0)
        def _():
            pbuf[0:HALO, :] = jnp.zeros((HALO, POOL_W), F32)

        pbuf[HALO:HALO + T, :] = p
        run = pbuf[...]
        width = 1
        sums = []
        for win in WINDOWS:
            while width < win:
                run = run + pltpu.roll(run, width, 0)
                width *= 2
            assert width == win and width - 1 <= HALO
            sums.append(run[HALO:, :LANE])
            run = run[:, LANE:]
        pos = j * T + lax.broadcasted_iota(I32, (T, 1), 0)
        ds = []
        for g, win in enumerate(WINDOWS):
            cnt = jnp.minimum(win, pos + 1).astype(F32)
            ds.append(sums[g] / cnt - p[:, g * LANE:(g + 1) * LANE])
        d = jnp.concatenate(ds, axis=-1)
        pbuf[0:HALO, :] = p[T - HALO:, :]

        x1, h2 = _mixer_back(x, ya, d, mod_ref, g2_ref, pw_ref, ps_ref, ga_ref, gb_ref, wout_ref, 1)
        x1_ref[...] = x1

        @pl.when(j == n_tiles - 1)
        def _():
            pstate_ref[...] = p[T - HALO:, :]
            cv_ref[...] = v[T - CHUNK:, :]

        if emit_route:
            local, route, seg_n, seg_0 = _route_sort(h2, rw_ref, u_ref)
            l_ref[...] = local
            rt_ref[...] = route
            segn_ref[...] = jnp.broadcast_to(seg_n, (N_EXP, LANE))
            seg0_ref[...] = jnp.broadcast_to(seg_0, (N_EXP, LANE))

    if emit_route:
        pl.when(i < n_rt)(tile)

        @pl.when(i >= n_rt)
        def _():
            l_ref[...] = jnp.zeros((LR, D), F32)
    else:
        tile()


def _mixer_sample_body(*refs, n_seq, n_pos, emit_route):
    (x_ref, mod_ref, buf_ref, g1_ref, g2_ref, win_ref, vg_ref, vb_ref, sw_ref, sb_ref,
     pw_ref, ps_ref, ga_ref, gb_ref, wout_ref) = refs[:15]
    k = 15
    if emit_route:
        rw_ref, u_ref = refs[k:k + 2]
        k += 3
    x1_ref = refs[k]
    k += 1
    if emit_route:
        l_ref, rt_ref, segn_ref, seg0_ref = refs[k:k + 4]
        k += 4
    pstate_ref, cv_ref = refs[k:k + 2]

    reps = n_pos
    x = x_ref[...]
    u, v, p = _mixer_front(x, mod_ref, g1_ref, win_ref, vg_ref, vb_ref, reps)

    def slab(a, t):
        return a[t * n_seq:(t + 1) * n_seq, :]

    mixed = []
    for t in range(n_pos):
        acc = sb_ref[t:t + 1, :] + sw_ref[t * n_pos:t * n_pos + 1, :] * slab(v, 0)
        for s in range(1, t + 1):
            acc = acc + sw_ref[t * n_pos + s:t * n_pos + s + 1, :] * slab(v, s)
        mixed.append(acc)
    ya = u * jnp.concatenate(mixed, axis=0)

    hist = [buf_ref[jj] for jj in range(POOL_BUF)] + [slab(p, t) for t in range(n_pos)]
    drows = []
    for t in range(n_pos):
        parts = []
        for g, win in enumerate(WINDOWS):
            lo = g * LANE
            ws = hist[POOL_BUF + t][:, lo:lo + LANE]
            for kk in range(1, win):
                ws = ws + hist[POOL_BUF + t - kk][:, lo:lo + LANE]
            cnt = float(min(win, PAST_LEN + t + 1))
            parts.append(ws / cnt - hist[POOL_BUF + t][:, lo:lo + LANE])
        drows.append(jnp.concatenate(parts, axis=-1))
    d = jnp.concatenate(drows, axis=0)
    for jj in range(POOL_BUF):
        pstate_ref[jj] = hist[n_pos + jj]
    for t in range(n_pos):
        cv_ref[t] = slab(v, t)

    x1, h2 = _mixer_back(x, ya, d, mod_ref, g2_ref, pw_ref, ps_ref, ga_ref, gb_ref, wout_ref, reps)
    x1_ref[...] = x1

    if emit_route:
        for i in range((n_seq * n_pos) // TR):
            local, route, seg_n, seg_0 = _route_sort(h2[i * TR:(i + 1) * TR, :], rw_ref, u_ref)
            l_ref[i * LR:(i + 1) * LR, :] = local
            rt_ref[:, i * TR:(i + 1) * TR] = route
            segn_ref[i] = jnp.broadcast_to(seg_n, (N_EXP, LANE))
            seg0_ref[i] = jnp.broadcast_to(seg_0, (N_EXP, LANE))


def _const_spec(shape):
    nd = len(shape)
    return pl.BlockSpec(shape, lambda *_: (0,) * nd)


def _rank_matrix():
    return jnp.triu(jnp.ones((TR, TR), BF16), 1)


def _mixer_prompt(x, mod, lw, route_w=None, n_rt_other=0):
    T = TR
    n_b = mod.shape[0]
    n_tok = x.shape[0]
    seq = n_tok // n_b
    n_tiles = seq // T
    n_rt = n_tok // T
    emit_route = route_w is not None
    n_steps = n_rt + n_rt_other + 1 if emit_route else n_rt
    tile = lambda i: jnp.minimum(i, n_rt - 1)
    in_specs = [
        pl.BlockSpec((T, D), lambda i: (tile(i), 0)),
        pl.BlockSpec((None, 1, 6 * D), lambda i: (tile(i) // n_tiles, 0, 0)),
    ] + [_const_spec(w.shape) for w in lw]
    args = [x, mod.reshape(n_b, 1, 6 * D)] + list(lw)
    out_shape = [jax.ShapeDtypeStruct((n_tok, D), F32)]
    out_specs = [pl.BlockSpec((T, D), lambda i: (tile(i), 0))]
    if emit_route:
        umat = _rank_matrix()
        in_specs += [_const_spec(route_w.shape), _const_spec(umat.shape)]
        args += [route_w, umat]
        out_shape += [jax.ShapeDtypeStruct((n_steps * LR, D), F32),
                      jax.ShapeDtypeStruct((N_EXP, n_tok), F32),
                      jax.ShapeDtypeStruct((n_rt, N_EXP, LANE), F32),
                      jax.ShapeDtypeStruct((n_rt, N_EXP, LANE), F32)]
        out_specs += [pl.BlockSpec((LR, D), lambda i: (i, 0)),
                      pl.BlockSpec((N_EXP, T), lambda i: (0, tile(i))),
                      pl.BlockSpec((None, N_EXP, LANE), lambda i: (tile(i), 0, 0)),
                      pl.BlockSpec((None, N_EXP, LANE), lambda i: (tile(i), 0, 0))]
    out_shape += [jax.ShapeDtypeStruct((n_b, HALO, POOL_W), F32),
                  jax.ShapeDtypeStruct((n_b, CHUNK, SGU_W), F32)]
    out_specs += [pl.BlockSpec((None, HALO, POOL_W), lambda i: (tile(i) // n_tiles, 0, 0)),
                  pl.BlockSpec((None, CHUNK, SGU_W), lambda i: (tile(i) // n_tiles, 0, 0))]
    return pl.pallas_call(
        functools.partial(_mixer_prompt_body, T=T, n_tiles=n_tiles, n_rt=n_rt, emit_route=emit_route),
        out_shape=out_shape,
        grid=(n_steps,),
        in_specs=in_specs,
        out_specs=out_specs,
        scratch_shapes=[pltpu.VMEM((HALO + T, POOL_W), F32)],
        compiler_params=pltpu.CompilerParams(
            dimension_semantics=("arbitrary",), vmem_limit_bytes=VMEM_LIMIT),
        name="mixer_prompt_route" if emit_route else "mixer_prompt",
    )(*args)


def _mixer_sample(x, mod, buf, lw, route_w=None, l_all=None, blk0=0):
    n_seq = mod.shape[0]
    n_tok = x.shape[0]
    n_pos = n_tok // n_seq
    n_rt = n_tok // TR
    emit_route = route_w is not None
    in_specs = [_const_spec(x.shape), _const_spec(mod.shape), _const_spec(buf.shape)]
    in_specs += [_const_spec(w.shape) for w in lw]
    args = [x, mod, buf] + list(lw)
    out_shape = [jax.ShapeDtypeStruct((n_tok, D), F32)]
    out_specs = [_const_spec((n_tok, D))]
    aliases = {}
    if emit_route:
        umat = _rank_matrix()
        in_specs += [_const_spec(route_w.shape), _const_spec(umat.shape), pl.BlockSpec(memory_space=pl.ANY)]
        args += [route_w, umat, l_all]
        aliases = {len(args) - 1: 1}
        assert blk0 % n_rt == 0
        out_shape += [jax.ShapeDtypeStruct(l_all.shape, F32),
                      jax.ShapeDtypeStruct((N_EXP, n_tok), F32),
                      jax.ShapeDtypeStruct((n_rt, N_EXP, LANE), F32),
                      jax.ShapeDtypeStruct((n_rt, N_EXP, LANE), F32)]
        out_specs += [pl.BlockSpec((n_rt * LR, D), lambda i: (blk0 // n_rt, 0)),
                      _const_spec((N_EXP, n_tok)),
                      _const_spec((n_rt, N_EXP, LANE)), _const_spec((n_rt, N_EXP, LANE))]
    out_shape += [jax.ShapeDtypeStruct((POOL_BUF, n_seq, POOL_W), F32),
                  jax.ShapeDtypeStruct((n_pos, n_seq, SGU_W), F32)]
    out_specs += [_const_spec((POOL_BUF, n_seq, POOL_W)), _const_spec((n_pos, n_seq, SGU_W))]
    return pl.pallas_call(
        functools.partial(_mixer_sample_body, n_seq=n_seq, n_pos=n_pos, emit_route=emit_route),
        out_shape=out_shape,
        grid=(1,),
        in_specs=in_specs,
        out_specs=out_specs,
        input_output_aliases=aliases,
        compiler_params=pltpu.CompilerParams(
            dimension_semantics=("arbitrary",), vmem_limit_bytes=VMEM_LIMIT),
        name="mixer_sample_route" if emit_route else "mixer_sample",
    )(*args)


def _ffn_body(x_ref, mod_ref, g2_ref, w1_ref, w3_ref, w2_ref, o_ref, *, reps):
    x1 = x_ref[...]
    sh2 = _mod_chunk(mod_ref, 3, reps)
    sc2 = _mod_chunk(mod_ref, 4, reps)
    h = (_rmsn(x1) * g2_ref[...] * (1.0 + sc2) + sh2).astype(BF16)
    acc = None
    for c in range(NF):
        a = _dot(h, w1_ref[:, c * TF:(c + 1) * TF])
        bb = _dot(h, w3_ref[:, c * TF:(c + 1) * TF])
        part = _dot((jax.nn.silu(a) * bb).astype(BF16), w2_ref[c * TF:(c + 1) * TF, :])
        acc = part if acc is None else acc + part
    o_ref[...] = x1 + _mod_chunk(mod_ref, 5, reps) * acc


def _mod_spec(mod, tile_rows, tok_per_mod):
    if tok_per_mod:
        spec = pl.BlockSpec((None, 1, 6 * D), lambda i, *_: ((i * tile_rows) // tok_per_mod, 0, 0))
        return mod.reshape(mod.shape[0], 1, 6 * D), spec, 1
    return mod, pl.BlockSpec(mod.shape, lambda i, *_: (0, 0)), tile_rows // mod.shape[0]


def _ffn(x1, mod, g2, w1, w3, w2, *, tm, tok_per_mod):
    n_tok = x1.shape[0]
    mod_in, mod_spec, reps = _mod_spec(mod, tm, tok_per_mod)
    resident = lambda s: pl.BlockSpec(s, lambda i: (0,) * len(s), pipeline_mode=pl.Buffered(1))
    return pl.pallas_call(
        functools.partial(_ffn_body, reps=reps),
        out_shape=jax.ShapeDtypeStruct((n_tok, D), F32),
        grid=(n_tok // tm,),
        in_specs=[pl.BlockSpec((tm, D), lambda i: (i, 0)), mod_spec, _const_spec(g2.shape),
                  resident(w1.shape), resident(w3.shape), resident(w2.shape)],
        out_specs=pl.BlockSpec((tm, D), lambda i: (i, 0)),
        compiler_params=pltpu.CompilerParams(
            dimension_semantics=("arbitrary",), vmem_limit_bytes=VMEM_LIMIT),
        name="ffn_dense",
    )(x1, mod_in, g2, w1, w3, w2)


def _moe_body(se_ref, row0_ref, nsub_ref, uin_ref, uout_ref, tl0_ref, tln_ref,
              l_ref, w1_ref, w3_ref, w2_ref, y_ref, xbuf, acc, sem_in, sem_out, *, n_tail):
    g = pl.program_id(0)
    f = pl.program_id(1)
    n = nsub_ref[g]
    u0 = row0_ref[g] // UNIT
    per_sub = SUBM // UNIT
    sub_shift = per_sub.bit_length() - 1
    batch = 4

    def rows(unit):
        return pl.ds(pl.multiple_of(unit * UNIT, UNIT), UNIT)

    def x_copy(unit, u, sem):
        return pltpu.make_async_copy(l_ref.at[rows(unit)], xbuf.at[rows(u)], sem)

    def y_copy(unit, src):
        return pltpu.make_async_copy(src, y_ref.at[rows(unit)], sem_out)

    def wait_rows(sem, count, m):
        def one(i, c):
            pltpu.make_async_copy(l_ref.at[pl.ds(0, m)], xbuf.at[pl.ds(0, m)], sem).wait()
            return c
        lax.fori_loop(0, count, one, 0)

    @pl.when((g == 0) & (f == 0))
    def _():
        acc[0:UNIT, :] = jnp.zeros((UNIT, D), F32)
        for phase in ("start", "wait"):
            def tile_tail(i, c):
                def one(k, c2):
                    getattr(y_copy(tl0_ref[i] + k, acc.at[pl.ds(0, UNIT)]), phase)()
                    return c2
                lax.fori_loop(0, tln_ref[i], one, 0)
                return c
            lax.fori_loop(0, n_tail, tile_tail, 0)

    @pl.when((f == 0) & (n > 0))
    def _():
        def start(i, c):
            for k in range(batch):
                u = i * batch + k
                x_copy(uin_ref[u0 + u], u, sem_in.at[lax.shift_right_logical(u, sub_shift)]).start(priority=1)
            return c
        lax.fori_loop(0, n * (per_sub // batch), start, 0)

    def scatter_start(sub, k):
        def start(i, c):
            for j in range(batch):
                u = sub * per_sub + i * batch + j
                y_copy(uout_ref[u0 + u], acc.at[rows(u)]).start(priority=1)
            return c
        lax.fori_loop(0, k * (per_sub // batch), start, 0)

    def run(mode):
        w1b = w1_ref[...].astype(BF16)
        w3b = w3_ref[...].astype(BF16)
        w2b = w2_ref[...].astype(BF16)

        def step(sub, k):
            if mode == "first":
                for j in range(k):
                    wait_rows(sem_in.at[sub + j], 1, SUBM)
            r = sub * SUBM if isinstance(sub, int) else pl.multiple_of(sub * SUBM, SUBM)
            m = k * SUBM
            h = xbuf[pl.ds(r, m), :].astype(BF16)
            a = _dot(h, w1b)
            bb = _dot(h, w3b)
            part = _dot((jax.nn.silu(a) * bb).astype(BF16), w2b)
            if mode == "first":
                acc[pl.ds(r, m), :] = part
            else:
                acc[pl.ds(r, m), :] = acc[pl.ds(r, m), :] + part
            if mode == "last":
                scatter_start(sub, k)

        def loop(count, first_sub, k):
            def body(i, c):
                step(first_sub + k * i, k)
                return c
            lax.fori_loop(0, count, body, 0)

        step(0, 1)
        rest = n - 1
        big = 4 if mode == "mid" else 2
        n_big = lax.shift_right_logical(rest, big.bit_length() - 1)
        loop(n_big, 1, big)
        done = 1 + n_big * big
        k = big // 2
        while k >= 1:
            @pl.when((rest & k) != 0)
            def _(k=k, done=done):
                step(done, k)
            done = done + (rest & k)
            k //= 2

        if mode == "last":
            wait_rows(sem_out, n, SUBM)

    @pl.when((f == 0) & (n > 0))
    def _():
        run("first")

    @pl.when((f > 0) & (f < NF - 1) & (n > 0))
    def _():
        run("mid")

    @pl.when((f == NF - 1) & (n > 0))
    def _():
        run("last")


def _moe_grouped(l_all, w1, w3, w2, plan):
    st_e = plan[0]
    n_st = st_e.shape[0]
    n_tail = plan[-1].shape[0]

    def fidx(g, f, ns):
        return jnp.where(ns[g] > 0, f, NF - 1)

    grid_spec = pltpu.PrefetchScalarGridSpec(
        num_scalar_prefetch=len(plan),
        grid=(n_st, NF),
        in_specs=[
            pl.BlockSpec(memory_space=pl.ANY),
            pl.BlockSpec((None, D, TF), lambda g, f, se, r0, ns, *_: (se[g], 0, fidx(g, f, ns))),
            pl.BlockSpec((None, D, TF), lambda g, f, se, r0, ns, *_: (se[g], 0, fidx(g, f, ns))),
            pl.BlockSpec((None, TF, D), lambda g, f, se, r0, ns, *_: (se[g], fidx(g, f, ns), 0)),
        ],
        out_specs=pl.BlockSpec(memory_space=pl.ANY),
        scratch_shapes=[pltpu.VMEM((SPT * SUBM, D), F32), pltpu.VMEM((SPT * SUBM, D), F32),
                        pltpu.SemaphoreType.DMA((SPT,)), pltpu.SemaphoreType.DMA(())],
    )
    return pl.pallas_call(
        functools.partial(_moe_body, n_tail=n_tail),
        out_shape=jax.ShapeDtypeStruct(l_all.shape, F32),
        grid_spec=grid_spec,
        compiler_params=pltpu.CompilerParams(
            dimension_semantics=("arbitrary", "arbitrary"), vmem_limit_bytes=VMEM_LIMIT,
            has_side_effects=True),
        name="moe_grouped",
    )(*plan, l_all, w1, w3, w2)


def _combine_body(rt_ref, x_ref, mod_ref, fg_ref, y_ref, o_ref, *, reps, final):
    tc = x_ref.shape[0]

    def column(k):
        return jnp.transpose(jnp.broadcast_to(rt_ref[k:k + 1, :], (LANE, tc)))

    lane_id = lax.broadcasted_iota(I32, (tc, LR), 1).astype(F32)
    yb = y_ref[...].astype(BF16)

    def pick(k):
        slot = jnp.concatenate([column(k)] * (LR // LANE), axis=1)
        onehot = jnp.where(lane_id == slot, 1.0, 0.0).astype(BF16)
        return _dot(onehot, yb)

    g1 = jnp.concatenate([column(2)] * (D // LANE), axis=1)
    g2 = jnp.concatenate([column(3)] * (D // LANE), axis=1)
    x2 = x_ref[...] + _mod_chunk(mod_ref, 5, reps) * (g1 * pick(0) + g2 * pick(1))
    o_ref[...] = _rmsn(x2) * fg_ref[...] if final else x2


def _combine(x1, mod, final_g, y_all, route, *, blk0, tok_per_mod, final):
    n_tok = x1.shape[0]
    tc = TR
    mod_in, mod_spec, reps = _mod_spec(mod, tc, tok_per_mod)
    return pl.pallas_call(
        functools.partial(_combine_body, reps=reps, final=final),
        out_shape=jax.ShapeDtypeStruct((n_tok, D), F32),
        grid=(n_tok // tc,),
        in_specs=[pl.BlockSpec((N_EXP, tc), lambda i: (0, i)),
                  pl.BlockSpec((tc, D), lambda i: (i, 0)),
                  mod_spec,
                  _const_spec((1, D)),
                  pl.BlockSpec((LR, D), lambda i: (blk0 + i, 0))],
        out_specs=pl.BlockSpec((tc, D), lambda i: (i, 0)),
        compiler_params=pltpu.CompilerParams(
            dimension_semantics=("arbitrary",), vmem_limit_bytes=VMEM_LIMIT),
        name="moe_combine",
    )(route, x1, mod_in, final_g, y_all)


def _final_norm_body(x_ref, fg_ref, o_ref):
    o_ref[...] = _rmsn(x_ref[...]) * fg_ref[...]


def _final_norm(x, final_g, *, tm=512):
    n_tok = x.shape[0]
    return pl.pallas_call(
        _final_norm_body,
        out_shape=jax.ShapeDtypeStruct((n_tok, D), F32),
        grid=(n_tok // tm,),
        in_specs=[pl.BlockSpec((tm, D), lambda i: (i, 0)), _const_spec((1, D))],
        out_specs=pl.BlockSpec((tm, D), lambda i: (i, 0)),
        name="final_norm",
    )(x, final_g)


def _expert_plan(seg_n, seg_0):
    n_rt = seg_n.shape[0]
    seg_n = seg_n.astype(I32)
    seg_0 = seg_0.astype(I32)
    total = jnp.sum(seg_n, axis=0)
    nsub = (total + SUBM - 1) // SUBM
    off = (jnp.cumsum(nsub) - nsub) * SUBM
    dst = off[None, :] + jnp.cumsum(seg_n, axis=0) - seg_n
    src = jnp.arange(n_rt, dtype=I32)[:, None] * LR + seg_0
    s_start = dst.T.reshape(-1) // UNIT
    s_len = seg_n.T.reshape(-1) // UNIT
    s_src = src.T.reshape(-1) // UNIT
    max_sub = (n_rt * (TOP_K * TR + N_EXP * (UNIT - 1))) // SUBM + N_EXP
    zero_unit = n_rt * LR // UNIT
    take = lambda table, idx: jnp.sum(
        jnp.where(idx[:, None] == jnp.arange(table.shape[0], dtype=I32)[None, :], table[None, :], 0), axis=1)
    unit = jnp.arange(max_sub * SUBM // UNIT, dtype=I32)
    owner = jnp.sum((s_start[None, :] <= unit[:, None]).astype(I32), axis=1) - 1
    owner = jnp.clip(owner, 0, s_start.shape[0] - 1)
    k = unit - take(s_start, owner)
    real = (k >= 0) & (k < take(s_len, owner))
    u_src = take(s_src, owner) + k
    u_in = jnp.where(real, u_src, zero_unit)
    u_out = jnp.where(real, u_src, zero_unit + 1 + unit % N_TRASH)
    nst = (nsub + SPT - 1) // SPT
    n_st = (max_sub + N_EXP * (SPT - 1)) // SPT
    st_end = jnp.cumsum(nst)
    g = jnp.arange(n_st, dtype=I32)
    n_used = st_end[-1]
    valid = g < n_used
    which = lambda q: jnp.minimum(jnp.sum((st_end[None, :] <= q[:, None]).astype(I32), axis=1), N_EXP - 1)
    e = jnp.where(valid, which(g), which((n_used - 1).reshape(1)))
    nst_e, nsub_e = take(nst, e), take(nsub, e)
    kk = g - (take(st_end, e) - nst_e)
    base = nsub_e // jnp.maximum(nst_e, 1)
    rem = nsub_e - base * nst_e
    size = jnp.where(valid, base + (kk < rem).astype(I32), 0)
    row0 = jnp.where(valid, take(off, e) + (kk * base + jnp.minimum(kk, rem)) * SUBM, 0)
    used = jnp.concatenate([jnp.sum(seg_n, axis=1), jnp.zeros((1,), I32)])
    tail0 = (jnp.arange(n_rt + 1, dtype=I32) * LR + used) // UNIT
    tailn = (LR - used) // UNIT
    as_i32 = lambda *xs: tuple(x.astype(I32) for x in xs)
    return as_i32(e, row0, size, u_in, u_out, tail0, tailn)


def _layer_weights(l, sample, norm1_g, norm2_g, w_in, v_norm_g, v_norm_b, sgu_w, sgu_b,
                   pool_w, pool_scale, branch_a_g, branch_b_g, w_out, n_pos):
    row = lambda a: a[l].reshape(1, -1)
    if sample:
        w8 = jnp.transpose(sgu_w[l][:, :n_pos, :n_pos], (1, 2, 0))
        sw = jnp.repeat(w8, HEAD_D, axis=2).reshape(n_pos * n_pos, SGU_W)
        sb = jnp.repeat(sgu_b[l][:, :n_pos].T, HEAD_D, axis=1)
    else:
        sw = sgu_w[l]
        sb = jnp.repeat(sgu_b[l].T, HEAD_D, axis=1)
    pw = pool_w[l].astype(BF16)
    z = jnp.zeros((LANE, LANE), BF16)
    pw2 = jnp.stack([jnp.block([[pw[0], z], [z, pw[1]]]), jnp.block([[pw[2], z], [z, pw[3]]])])
    return [row(norm1_g), row(norm2_g), w_in[l].astype(BF16), row(v_norm_g), row(v_norm_b), sw, sb,
            pw2, row(pool_scale), row(branch_a_g), row(branch_b_g), w_out[l].astype(BF16)]


def kernel(x_prompt, x_sample, state_pool, c_prompt, c_sample, norm1_g, norm2_g, ada_w, ada_b, w_in, v_norm_g, v_norm_b, sgu_w, sgu_b, pool_w, pool_scale, branch_a_g, branch_b_g, w_out, ffn_w1, ffn_w3, ffn_w2, router_w, moe_w1, moe_w3, moe_w2, final_g):
    n_b, seq, _ = x_prompt.shape
    n_seq, n_pos, _ = x_sample.shape
    depth = ada_w.shape[0]
    n_p = n_b * seq
    n_s = n_seq * n_pos

    mod = _adaln(jnp.concatenate([c_prompt, c_sample], axis=0), ada_w, ada_b)
    mod_p, mod_s = mod[:, :n_b], mod[:, n_b:]

    xp = x_prompt.reshape(n_p, D)
    xs = jnp.swapaxes(x_sample, 0, 1).reshape(n_s, D)
    bufs = jnp.transpose(state_pool, (0, 2, 1, 3))
    fg = final_g.reshape(1, D)
    per_layer = (norm1_g, norm2_g, w_in, v_norm_g, v_norm_b, sgu_w, sgu_b, pool_w, pool_scale,
                 branch_a_g, branch_b_g, w_out)

    pool_p, pool_s, cv_p, cv_s = [], [], [], []
    normed = False
    for l in range(depth):
        lw_p = _layer_weights(l, False, *per_layer, n_pos)
        lw_s = _layer_weights(l, True, *per_layer, n_pos)
        g2 = norm2_g[l].reshape(1, D)
        if l % 2 == 0:
            x1p, pp, vp = _mixer_prompt(xp, mod_p[l], lw_p)
            x1s, ps, vs = _mixer_sample(xs, mod_s[l], bufs[l], lw_s)
            w1, w3, w2 = (w[l // 2].astype(BF16) for w in (ffn_w1, ffn_w3, ffn_w2))
            xp = _ffn(x1p, mod_p[l], g2, w1, w3, w2, tm=1024, tok_per_mod=seq)
            xs = _ffn(x1s, mod_s[l], g2, w1, w3, w2, tm=2 * n_seq, tok_per_mod=0)
            normed = False
        else:
            m = l // 2
            rw = jnp.zeros((2 * N_EXP, D), BF16).at[:N_EXP].set(router_w[m].T.astype(BF16))
            rt_tiles_p, rt_tiles_s = n_p // TR, n_s // TR
            x1p, l_all, rt_p, segn_p, seg0_p, pp, vp = _mixer_prompt(
                xp, mod_p[l], lw_p, route_w=rw, n_rt_other=rt_tiles_s)
            x1s, l_all, rt_s, segn_s, seg0_s, ps, vs = _mixer_sample(
                xs, mod_s[l], bufs[l], lw_s, route_w=rw, l_all=l_all, blk0=rt_tiles_p)
            plan = _expert_plan(jnp.concatenate([segn_p[:, :, 0], segn_s[:, :, 0]], axis=0),
                                jnp.concatenate([seg0_p[:, :, 0], seg0_s[:, :, 0]], axis=0))
            y_all = _moe_grouped(l_all, moe_w1[m], moe_w3[m], moe_w2[m], plan)
            normed = l == depth - 1
            xp = _combine(x1p, mod_p[l], fg, y_all, rt_p, blk0=0, tok_per_mod=seq, final=normed)
            xs = _combine(x1s, mod_s[l], fg, y_all, rt_s, blk0=rt_tiles_p, tok_per_mod=0, final=normed)
        pool_p.append(pp[:, HALO - POOL_BUF:])
        pool_s.append(jnp.swapaxes(ps, 0, 1))
        cv_p.append(vp)
        cv_s.append(jnp.swapaxes(vs, 0, 1))

    if not normed:
        xp = _final_norm(xp, fg)
        xs = _final_norm(xs, fg)
    y_prompt = xp.reshape(n_b, seq, D)
    y_sample = jnp.swapaxes(xs.reshape(n_pos, n_seq, D), 0, 1)
    return (y_prompt, y_sample, jnp.stack(pool_p), jnp.stack(pool_s), jnp.stack(cv_p), jnp.stack(cv_s))
```
